```python
import math
import jax, jax.numpy as jnp
from jax import lax
import numpy as np

D_MODEL = 1024
BATCH = 8
SEQ = 2048
DEPTH = 4
DEC_BATCH = 4
DEC_SEQ = 8192
PAST_LEN = 128

GRID_W = 64
NA_HEADS = 16
NA_HEAD_DIM = D_MODEL // NA_HEADS
NA_WIN_H = 8
NA_WIN_W = 16
SW_HEADS = 16
SW_KV_HEADS = 4
SW_HEAD_DIM = D_MODEL // SW_HEADS
SW_GROUPS = SW_HEADS // SW_KV_HEADS
SW_WINDOW = 128
SW_BLOCK = 128
DIFF_HEADS = 8
DIFF_HEAD_DIM = D_MODEL // DIFF_HEADS // 2
Q_BLOCK = 128
N_EXPERTS = 16
EC_CAPACITY_FACTOR = 2
D_EXPERT = 2048

ROPE_THETA = 10000.0
NORM_EPS = 1e-6
MIXER_PATTERN = ("nat", "swa", "diff")

kernel_name = "hybrid_nat_swa_diff_ec_encoder"


def rmsnorm(x, g):
    x32 = x.astype(jnp.float32)
    y = x32 * lax.rsqrt(jnp.mean(x32 * x32, axis=-1, keepdims=True) + NORM_EPS)
    return (y * g.astype(jnp.float32)).astype(x.dtype)


def rope(x, pos):
    hd = x.shape[-1]
    inv = ROPE_THETA ** (-jnp.arange(0, hd, 2, dtype=jnp.float32) / hd)
    ang = pos[:, None] * inv[None, :]
    cos = jnp.cos(ang)[:, None, :]
    sin = jnp.sin(ang)[:, None, :]
    x32 = x.astype(jnp.float32)
    x1, x2 = x32[..., : hd // 2], x32[..., hd // 2:]
    return jnp.concatenate([x1 * cos - x2 * sin, x2 * cos + x1 * sin], axis=-1).astype(x.dtype)


def neighborhood_attention(h, w_qkv, rpb, w_o):
    bsz, t, _ = h.shape
    rows = t // GRID_W
    kh = min(NA_WIN_H, rows)
    q, k, v = jnp.split(h @ w_qkv, 3, axis=-1)
    grid = (bsz, rows, GRID_W, NA_HEADS, NA_HEAD_DIM)
    q = q.reshape(grid) * (NA_HEAD_DIM ** -0.5)
    k = k.reshape(grid)
    v = v.reshape(grid)
    cols = jnp.arange(GRID_W)
    col_start = jnp.clip(cols - NA_WIN_W // 2, 0, GRID_W - NA_WIN_W)
    col_in = (cols[None, :] >= col_start[:, None]) & (cols[None, :] < col_start[:, None] + NA_WIN_W)
    dc_idx = jnp.clip(cols[None, :] - cols[:, None] + NA_WIN_W - 1, 0, 2 * NA_WIN_W - 2)

    def row_fn(r):
        rs = jnp.clip(r - kh // 2, 0, rows - kh)
        q_r = lax.dynamic_index_in_dim(q, r, axis=1, keepdims=False)
        k_b = lax.dynamic_slice_in_dim(k, rs, kh, axis=1)
        v_b = lax.dynamic_slice_in_dim(v, rs, kh, axis=1)
        dr_idx = rs + jnp.arange(kh) - r + NA_WIN_H - 1
        bias = rpb[:, dr_idx][:, :, dc_idx]
        s = jnp.einsum("bqhd,bjkhd->bhqjk", q_r, k_b).astype(jnp.float32)
        s = s + jnp.transpose(bias, (0, 2, 1, 3)).astype(jnp.float32)[None]
        s = jnp.where(col_in[None, None, :, None, :], s, -jnp.inf)
        p = jax.nn.softmax(s.reshape(bsz, NA_HEADS, GRID_W, kh * GRID_W), axis=-1).reshape(s.shape)
        return jnp.einsum("bhqjk,bjkhd->bqhd", p.astype(v.dtype), v_b)

    o = lax.map(row_fn, jnp.arange(rows))
    o = jnp.transpose(o, (1, 0, 2, 3, 4)).reshape(bsz, t, D_MODEL)
    return o @ w_o


def window_gqa_sink(h, w_qkv, sink, w_o, pos):
    bsz, t, _ = h.shape
    nblk = t // SW_BLOCK
    qkv = h @ w_qkv
    qd = SW_HEADS * SW_HEAD_DIM
    kd = SW_KV_HEADS * SW_HEAD_DIM
    q = qkv[..., :qd].reshape(bsz, t, SW_HEADS, SW_HEAD_DIM)
    k = qkv[..., qd:qd + kd].reshape(bsz, t, SW_KV_HEADS, SW_HEAD_DIM)
    v = qkv[..., qd + kd:].reshape(bsz, t, SW_KV_HEADS, SW_HEAD_DIM)
    q = (rope(q, pos) * (SW_HEAD_DIM ** -0.5)).reshape(bsz, t, SW_KV_HEADS, SW_GROUPS, SW_HEAD_DIM)
    k = rope(k, pos)
    padw = ((0, 0), (SW_BLOCK, SW_BLOCK), (0, 0), (0, 0))
    k_p = jnp.pad(k, padw)
    v_p = jnp.pad(v, padw)
    offs_q = jnp.arange(SW_BLOCK)
    offs_k = jnp.arange(3 * SW_BLOCK) - SW_BLOCK
    band = jnp.abs(offs_k[None, :] - offs_q[:, None]) <= SW_WINDOW
    sink_l = sink.reshape(SW_KV_HEADS, SW_GROUPS).astype(jnp.float32)[None, :, :, None, None]

    def blk(i):
        start = i * SW_BLOCK
        q_b = lax.dynamic_slice_in_dim(q, start, SW_BLOCK, axis=1)
        k_b = lax.dynamic_slice_in_dim(k_p, start, 3 * SW_BLOCK, axis=1)
        v_b = lax.dynamic_slice_in_dim(v_p, start, 3 * SW_BLOCK, axis=1)
        key_pos = start + offs_k
        valid = band & ((key_pos >= 0) & (key_pos < t))[None, :]
        s = jnp.einsum("bqkgd,bskd->bkgqs", q_b, k_b).astype(jnp.float32)
        s = jnp.where(valid, s, -jnp.inf)
        s = jnp.concatenate([s, jnp.broadcast_to(sink_l, s.shape[:-1] + (1,))], axis=-1)
        p = jax.nn.softmax(s, axis=-1)[..., :-1]
        o = jnp.einsum("bkgqs,bskd->bqkgd", p.astype(v.dtype), v_b)
        return o.reshape(bsz, SW_BLOCK, SW_HEADS * SW_HEAD_DIM)

    o = lax.map(blk, jnp.arange(nblk))
    o = jnp.transpose(o, (1, 0, 2, 3)).reshape(bsz, t, D_MODEL)
    return o @ w_o


def diff_attention(h, w_qkv, lam_q1, lam_k1, lam_q2, lam_k2, subln, w_o, pos, lambda_init):
    bsz, t, _ = h.shape
    nblk = t // Q_BLOCK
    q, k, v = jnp.split(h @ w_qkv, 3, axis=-1)
    q = rope(q.reshape(bsz, t, 2 * DIFF_HEADS, DIFF_HEAD_DIM), pos) * (DIFF_HEAD_DIM ** -0.5)
    k = rope(k.reshape(bsz, t, 2 * DIFF_HEADS, DIFF_HEAD_DIM), pos)
    q = q.reshape(bsz, t, DIFF_HEADS, 2, DIFF_HEAD_DIM)
    k = k.reshape(bsz, t, DIFF_HEADS, 2, DIFF_HEAD_DIM)
    v = v.reshape(bsz, t, DIFF_HEADS, 2 * DIFF_HEAD_DIM)
    lam = (jnp.exp(jnp.sum(lam_q1.astype(jnp.float32) * lam_k1.astype(jnp.float32)))
           - jnp.exp(jnp.sum(lam_q2.astype(jnp.float32) * lam_k2.astype(jnp.float32)))
           + lambda_init)

    def blk(i):
        q_b = lax.dynamic_slice_in_dim(q, i * Q_BLOCK, Q_BLOCK, axis=1)
        s = jnp.einsum("bqhcd,bkhcd->bhcqk", q_b, k).astype(jnp.float32)
        p = jax.nn.softmax(s, axis=-1)
        a = p[:, :, 0] - lam * p[:, :, 1]
        return jnp.einsum("bhqk,bkhd->bqhd", a.astype(v.dtype), v)

    o = lax.map(blk, jnp.arange(nblk))
    o = jnp.transpose(o, (1, 0, 2, 3, 4)).reshape(bsz, t, DIFF_HEADS, 2 * DIFF_HEAD_DIM)
    o = rmsnorm(o, subln) * (1.0 - lambda_init)
    return o.reshape(bsz, t, D_MODEL) @ w_o


def expert_choice_moe(h, w_router, w_gate, w_up, w_down):
    bsz, t, d = h.shape
    n = bsz * t
    cap = EC_CAPACITY_FACTOR * n // N_EXPERTS
    xf = h.reshape(n, d)
    aff = jax.nn.softmax((xf @ w_router).astype(jnp.float32), axis=-1)
    gate, idx = lax.top_k(aff.T, cap)
    xe = xf[idx]
    hid = jax.nn.silu(jnp.einsum("ecd,edf->ecf", xe, w_gate)) * jnp.einsum("ecd,edf->ecf", xe, w_up)
    ye = jnp.einsum("ecf,efd->ecd", hid, w_down) * gate[..., None].astype(xf.dtype)
    out = jnp.zeros_like(xf).at[idx.reshape(-1)].add(ye.reshape(-1, d))
    return out.reshape(bsz, t, d)


def setup_inputs(seed: int = 0) -> dict:
    key = jax.random.key(seed)
    keys = iter(jax.random.split(key, 128))

    def nrm(shape, scale):
        return jax.random.normal(next(keys), shape, jnp.float32) * scale

    def gain(shape):
        return 1.0 + nrm(shape, 0.02)

    inp = {
        "x_prompt": nrm((BATCH, SEQ, D_MODEL), 1.0),
        "x_sample": nrm((DEC_BATCH, DEC_SEQ, D_MODEL), 1.0),
    }
    for i in range(DEPTH):
        kind = MIXER_PATTERN[i % len(MIXER_PATTERN)]
        p = f"l{i}_"
        inp[p + "mix_norm"] = gain((D_MODEL,))
        if kind == "nat":
            inp[p + "w_qkv"] = nrm((D_MODEL, 3 * D_MODEL), D_MODEL ** -0.5)
            inp[p + "rpb"] = nrm((NA_HEADS, 2 * NA_WIN_H - 1, 2 * NA_WIN_W - 1), 0.1)
            inp[p + "w_o"] = nrm((D_MODEL, D_MODEL), D_MODEL ** -0.5)
        elif kind == "swa":
            inp[p + "w_qkv"] = nrm((D_MODEL, (SW_HEADS + 2 * SW_KV_HEADS) * SW_HEAD_DIM), D_MODEL ** -0.5)
            inp[p + "sink"] = nrm((SW_HEADS,), 0.5)
            inp[p + "w_o"] = nrm((D_MODEL, D_MODEL), D_MODEL ** -0.5)
        else:
            inp[p + "w_qkv"] = nrm((D_MODEL, 3 * D_MODEL), D_MODEL ** -0.5)
            inp[p + "lambda_q1"] = nrm((DIFF_HEAD_DIM,), 0.1)
            inp[p + "lambda_k1"] = nrm((DIFF_HEAD_DIM,), 0.1)
            inp[p + "lambda_q2"] = nrm((DIFF_HEAD_DIM,), 0.1)
            inp[p + "lambda_k2"] = nrm((DIFF_HEAD_DIM,), 0.1)
            inp[p + "subln"] = gain((2 * DIFF_HEAD_DIM,))
            inp[p + "w_o"] = nrm((D_MODEL, D_MODEL), D_MODEL ** -0.5)
        inp[p + "ffn_norm"] = gain((D_MODEL,))
        inp[p + "w_router"] = nrm((D_MODEL, N_EXPERTS), D_MODEL ** -0.5)
        inp[p + "w_gate"] = nrm((N_EXPERTS, D_MODEL, D_EXPERT), D_MODEL ** -0.5)
        inp[p + "w_up"] = nrm((N_EXPERTS, D_MODEL, D_EXPERT), D_MODEL ** -0.5)
        inp[p + "w_down"] = nrm((N_EXPERTS, D_EXPERT, D_MODEL), D_EXPERT ** -0.5)
    inp["final_norm"] = gain((D_MODEL,))
    return inp


def reference(x_prompt, x_sample,
              l0_mix_norm, l0_w_qkv, l0_rpb, l0_w_o,
              l0_ffn_norm, l0_w_router, l0_w_gate, l0_w_up, l0_w_down,
              l1_mix_norm, l1_w_qkv, l1_sink, l1_w_o,
              l1_ffn_norm, l1_w_router, l1_w_gate, l1_w_up, l1_w_down,
              l2_mix_norm, l2_w_qkv, l2_lambda_q1, l2_lambda_k1, l2_lambda_q2, l2_lambda_k2, l2_subln, l2_w_o,
              l2_ffn_norm, l2_w_router, l2_w_gate, l2_w_up, l2_w_down,
              l3_mix_norm, l3_w_qkv, l3_rpb, l3_w_o,
              l3_ffn_norm, l3_w_router, l3_w_gate, l3_w_up, l3_w_down,
              final_norm):
    mixers = (
        (l0_mix_norm, (l0_w_qkv, l0_rpb, l0_w_o)),
        (l1_mix_norm, (l1_w_qkv, l1_sink, l1_w_o)),
        (l2_mix_norm, (l2_w_qkv, l2_lambda_q1, l2_lambda_k1, l2_lambda_q2, l2_lambda_k2, l2_subln, l2_w_o)),
        (l3_mix_norm, (l3_w_qkv, l3_rpb, l3_w_o)),
    )
    ffns = (
        (l0_ffn_norm, (l0_w_router, l0_w_gate, l0_w_up, l0_w_down)),
        (l1_ffn_norm, (l1_w_router, l1_w_gate, l1_w_up, l1_w_down)),
        (l2_ffn_norm, (l2_w_router, l2_w_gate, l2_w_up, l2_w_down)),
        (l3_ffn_norm, (l3_w_router, l3_w_gate, l3_w_up, l3_w_down)),
    )

    def run(x):
        t = x.shape[1]
        pos = jnp.arange(t, dtype=jnp.float32)
        for i in range(DEPTH):
            kind = MIXER_PATTERN[i % len(MIXER_PATTERN)]
            mix_norm, mp = mixers[i]
            h = rmsnorm(x, mix_norm)
            if kind == "nat":
                x = x + neighborhood_attention(h, *mp)
            elif kind == "swa":
                x = x + window_gqa_sink(h, *mp, pos)
            else:
                lambda_init = 0.8 - 0.6 * math.exp(-0.3 * i)
                x = x + diff_attention(h, *mp, pos, lambda_init)
            ffn_norm, fp = ffns[i]
            x = x + expert_choice_moe(rmsnorm(x, ffn_norm), *fp)
        return rmsnorm(x, final_norm)

    y_prompt = run(x_prompt)
    y_sample = run(x_sample)
    return (y_prompt, y_sample)
```

```python
import functools
import math

import jax
import jax.numpy as jnp
from jax import lax
from jax.experimental import pallas as pl
from jax.experimental.pallas import tpu as pltpu

D_MODEL = 1024
DEPTH = 4
GRID_W = 64
NA_HEADS = 16
NA_WIN_H = 8
NA_WIN_W = 16
SW_HEADS = 16
SW_KV_HEADS = 4
SW_BLOCK = 128
DIFF_HEADS = 8
HEAD_DIM = 64
N_EXPERTS = 16
EC_CAPACITY_FACTOR = 2
D_EXPERT = 2048
ROPE_THETA = 10000.0
NORM_EPS = 1e-6
MIXER_PATTERN = ("nat", "swa", "diff")

LANES = 128
VMEM_LIMIT = 56 * 1024 * 1024

BF16 = jnp.bfloat16
F32 = jnp.float32
NEG_INF = float("-inf")


def _cparams(sem):
    return pltpu.CompilerParams(dimension_semantics=sem, vmem_limit_bytes=VMEM_LIMIT)


def _dot(a, b):
    return jnp.dot(a, b, preferred_element_type=F32)


def _dot_nt(a, b):
    return lax.dot_general(a, b, (((1,), (1,)), ((), ())), preferred_element_type=F32)


def _rms(x, g):
    return x * lax.rsqrt(jnp.mean(x * x, axis=-1, keepdims=True) + NORM_EPS) * g


PROJ_TM = 256


def _qkv_body(x_ref, g_ref, w_ref, cos_ref, sin_ref, q_ref, k_ref, v_ref, *, use_rope):
    h = _rms(x_ref[...], g_ref[...])
    y = _dot(h.astype(BF16), w_ref[...])
    tm = y.shape[0]
    lane = lax.broadcasted_iota(jnp.int32, (tm, LANES), 1)
    first_half = (lane % HEAD_DIM) < (HEAD_DIM // 2)
    if use_rope:
        cos = cos_ref[...]
        sin = sin_ref[...]
    outs = (q_ref, k_ref, v_ref)
    for part in range(3):
        for c in range(D_MODEL // LANES):
            col = part * D_MODEL + c * LANES
            blk = y[:, col:col + LANES]
            if use_rope and part < 2:
                rot = jnp.where(first_half,
                                pltpu.roll(blk, LANES - HEAD_DIM // 2, 1),
                                pltpu.roll(blk, HEAD_DIM // 2, 1))
                blk = blk * cos + rot * sin
            if part == 0:
                blk = blk * (HEAD_DIM ** -0.5)
            outs[part][:, c * LANES:(c + 1) * LANES] = blk.astype(BF16)


def _qkv_proj(x, g, w, cos, sin, seq_len, use_rope):
    n = x.shape[0]
    tm = PROJ_TM
    tblocks = seq_len // tm
    out = jax.ShapeDtypeStruct((n, D_MODEL), BF16)
    row = lambda i: (i, 0)
    return pl.pallas_call(
        functools.partial(_qkv_body, use_rope=use_rope),
        grid=(n // tm,),
        in_specs=[
            pl.BlockSpec((tm, D_MODEL), row),
            pl.BlockSpec((1, D_MODEL), lambda i: (0, 0)),
            pl.BlockSpec((D_MODEL, 3 * D_MODEL), lambda i: (0, 0)),
            pl.BlockSpec((tm, LANES), lambda i: (i % tblocks, 0)),
            pl.BlockSpec((tm, LANES), lambda i: (i % tblocks, 0)),
        ],
        out_specs=[pl.BlockSpec((tm, D_MODEL), row)] * 3,
        out_shape=[out, out, out],
        compiler_params=_cparams(("parallel",)),
        name="qkv_proj",
    )(x, g, w, cos, sin)


def _rope_tables(seq_len):
    inv = ROPE_THETA ** (-jnp.arange(0, HEAD_DIM, 2, dtype=F32) / HEAD_DIM)
    ang = jnp.arange(seq_len, dtype=F32)[:, None] * inv[None, :]
    cos = jnp.cos(ang)
    sin = jnp.sin(ang)
    cos_t = jnp.tile(jnp.concatenate([cos, cos], axis=1), (1, LANES // HEAD_DIM))
    sin_t = jnp.tile(jnp.concatenate([-sin, sin], axis=1), (1, LANES // HEAD_DIM))
    return cos_t, sin_t


def _pair_split(qp):
    lane = lax.broadcasted_iota(jnp.int32, qp.shape, 1)
    zero = jnp.zeros_like(qp)
    return jnp.concatenate([jnp.where(lane < HEAD_DIM, qp, zero),
                            jnp.where(lane >= HEAD_DIM, qp, zero)], axis=0)


def _pair_merge(o, m):
    lane = lax.broadcasted_iota(jnp.int32, (m, LANES), 1)
    return jnp.where(lane < HEAD_DIM, o[:m], o[m:])


NAT_ROWS_PER_STEP = 4
NAT_TQ = NAT_ROWS_PER_STEP * GRID_W


def _nat_body(q_ref, k0_ref, k1_ref, k2_ref, v0_ref, v1_ref, v2_ref, b_ref, o_ref, kwin, vwin,
              *, rows):
    i = pl.program_id(2)
    nb = rows // NAT_ROWS_PER_STEP
    wb = jnp.clip(i - 1, 0, nb - 3)
    for j, (kr, vr) in enumerate(((k0_ref, v0_ref), (k1_ref, v1_ref), (k2_ref, v2_ref))):
        kwin[j * NAT_TQ:(j + 1) * NAT_TQ, :] = kr[...]
        vwin[j * NAT_TQ:(j + 1) * NAT_TQ, :] = vr[...]
    win_keys = NA_WIN_H * GRID_W
    for a in range(NAT_ROWS_PER_STEP):
        r = i * NAT_ROWS_PER_STEP + a
        rs = jnp.clip(r - NA_WIN_H // 2, 0, rows - NA_WIN_H)
        j0 = pl.multiple_of((rs - wb * NAT_ROWS_PER_STEP) * GRID_W, GRID_W)
        d0 = rs - r + NA_WIN_H - 1
        kw = kwin[pl.ds(j0, win_keys), :]
        vw = vwin[pl.ds(j0, win_keys), :]
        qs = _pair_split(q_ref[a * GRID_W:(a + 1) * GRID_W, :])
        s = _dot_nt(qs, kw) + b_ref[0, d0]
        m = jnp.max(s, axis=-1, keepdims=True)
        e = jnp.exp(s - m)
        l = jnp.sum(e, axis=-1, keepdims=True)
        p = e / l
        o = _dot(p.astype(BF16), vw)
        o_ref[a * GRID_W:(a + 1) * GRID_W, :] = _pair_merge(o, GRID_W).astype(BF16)


def _nat_bias_table(rpb):
    cols = jnp.arange(GRID_W)
    col_start = jnp.clip(cols - NA_WIN_W // 2, 0, GRID_W - NA_WIN_W)
    col_in = (cols[None, :] >= col_start[:, None]) & (cols[None, :] < col_start[:, None] + NA_WIN_W)
    dc_idx = jnp.clip(cols[None, :] - cols[:, None] + NA_WIN_W - 1, 0, 2 * NA_WIN_W - 2)
    tz = jnp.where(col_in[None, None], rpb.astype(F32)[:, :, dc_idx], NEG_INF)
    dr = jnp.arange(NA_WIN_H)[:, None] + jnp.arange(NA_WIN_H)[None, :]
    t = tz[:, dr]
    t = jnp.transpose(t, (0, 1, 3, 2, 4)).reshape(NA_HEADS, NA_WIN_H, GRID_W, NA_WIN_H * GRID_W)
    t = t.reshape(NA_HEADS // 2, 2, NA_WIN_H, GRID_W, NA_WIN_H * GRID_W)
    t = jnp.transpose(t, (0, 2, 1, 3, 4))
    return t.reshape(NA_HEADS // 2, NA_WIN_H, 2 * GRID_W, NA_WIN_H * GRID_W)


def _nat_attention(q, k, v, rpb, bsz, seq_len):
    n = q.shape[0]
    rows = seq_len // GRID_W
    nb = rows // NAT_ROWS_PER_STEP
    assert rows % NAT_ROWS_PER_STEP == 0 and nb >= 3 and rows >= NA_WIN_H
    bias = _nat_bias_table(rpb)
    pairs = NA_HEADS // 2

    def win(j):
        return lambda hp, b, i: (b * nb + jnp.clip(i - 1, 0, nb - 3) + j, hp)

    qmap = lambda hp, b, i: (b * nb + i, hp)
    blk = lambda m: pl.BlockSpec((NAT_TQ, LANES), m)
    return pl.pallas_call(
        functools.partial(_nat_body, rows=rows),
        grid=(pairs, bsz, nb),
        in_specs=[blk(qmap), blk(win(0)), blk(win(1)), blk(win(2)),
                  blk(win(0)), blk(win(1)), blk(win(2)),
                  pl.BlockSpec((1, NA_WIN_H, 2 * GRID_W, NA_WIN_H * GRID_W),
                               lambda hp, b, i: (hp, 0, 0, 0))],
        out_specs=blk(qmap),
        out_shape=jax.ShapeDtypeStruct((n, D_MODEL), BF16),
        scratch_shapes=[pltpu.VMEM((3 * NAT_TQ, LANES), BF16),
                        pltpu.VMEM((3 * NAT_TQ, LANES), BF16)],
        compiler_params=_cparams(("parallel", "parallel", "parallel")),
        name="nat_attention",
    )(q, k, k, k, v, v, v, bias)


def _swa_body(sink_ref, q_ref, k0_ref, k1_ref, k2_ref, v0_ref, v1_ref, v2_ref, o_ref, *, nblk):
    i = pl.program_id(1)
    tb = SW_BLOCK
    qq = lax.broadcasted_iota(jnp.int32, (2 * tb, tb), 0) % tb
    kk = lax.broadcasted_iota(jnp.int32, (2 * tb, tb), 1)
    valid_prev = (kk >= qq) & (i > 0)
    valid_next = (kk <= qq) & (i < nblk - 1)
    row = lax.broadcasted_iota(jnp.int32, (2 * tb, 1), 0)
    for w in range(SW_HEADS // 2):
        cs = slice(w * LANES, (w + 1) * LANES)
        qs = _pair_split(q_ref[:, cs])
        s0 = jnp.where(valid_prev, _dot_nt(qs, k0_ref[:, cs]), NEG_INF)
        s1 = _dot_nt(qs, k1_ref[:, cs])
        s2 = jnp.where(valid_next, _dot_nt(qs, k2_ref[:, cs]), NEG_INF)
        sink = jnp.where(row < tb, sink_ref[2 * w], sink_ref[2 * w + 1])
        m = jnp.maximum(jnp.maximum(jnp.max(s0, axis=-1, keepdims=True),
                                    jnp.max(s1, axis=-1, keepdims=True)),
                        jnp.maximum(jnp.max(s2, axis=-1, keepdims=True), sink))
        e0 = jnp.exp(s0 - m)
        e1 = jnp.exp(s1 - m)
        e2 = jnp.exp(s2 - m)
        l = (jnp.sum(e0, axis=-1, keepdims=True) + jnp.sum(e1, axis=-1, keepdims=True)
             + jnp.sum(e2, axis=-1, keepdims=True) + jnp.exp(sink - m))
        o = (_dot((e0 / l).astype(BF16), v0_ref[:, cs]) + _dot((e1 / l).astype(BF16), v1_ref[:, cs])
             + _dot((e2 / l).astype(BF16), v2_ref[:, cs]))
        o_ref[:, cs] = _pair_merge(o, tb).astype(BF16)


def _swa_attention(q, k, v, sink, bsz, seq_len):
    n = q.shape[0]
    nblk = seq_len // SW_BLOCK
    cur = lambda b, i: (b * nblk + i, 0)
    prev = lambda b, i: (b * nblk + jnp.maximum(i - 1, 0), 0)
    nxt = lambda b, i: (b * nblk + jnp.minimum(i + 1, nblk - 1), 0)
    blk = lambda m: pl.BlockSpec((SW_BLOCK, D_MODEL), m)
    return pl.pallas_call(
        functools.partial(_swa_body, nblk=nblk),
        grid=(bsz, nblk),
        in_specs=[pl.BlockSpec(memory_space=pltpu.SMEM),
                  blk(cur), blk(prev), blk(cur), blk(nxt), blk(prev), blk(cur), blk(nxt)],
        out_specs=blk(cur),
        out_shape=jax.ShapeDtypeStruct((n, D_MODEL), BF16),
        compiler_params=_cparams(("parallel", "parallel")),
        name="swa_attention",
    )(sink.astype(F32), q, k, k, k, v, v, v)


DIFF_TQ = 1024
DIFF_TK = 512


def _diff_body(q_ref, k_ref, v_ref, lam_ref, subln_ref, o_ref, m_sc, l_sc, acc_sc, *, lambda_init):
    kv = pl.program_id(3)
    tq = q_ref.shape[0]

    @pl.when(kv == 0)
    def _():
        m_sc[...] = jnp.full(m_sc.shape, NEG_INF, F32)
        l_sc[...] = jnp.zeros(l_sc.shape, F32)
        acc_sc[...] = jnp.zeros(acc_sc.shape, F32)

    qs = _pair_split(q_ref[...])
    s = _dot_nt(qs, k_ref[...])
    m_prev = m_sc[...]
    m_new = jnp.maximum(m_prev, jnp.max(s, axis=-1, keepdims=True))
    alpha = jnp.exp(m_prev - m_new)
    p = jnp.exp(s - m_new)
    l_sc[...] = alpha * l_sc[...] + jnp.sum(p, axis=-1, keepdims=True)
    acc_sc[...] = alpha * acc_sc[...] + _dot(p.astype(BF16), v_ref[...])
    m_sc[...] = m_new

    @pl.when(kv == pl.num_programs(3) - 1)
    def _():
        o12 = acc_sc[...] / l_sc[...]
        lam4 = lam_ref[...]
        lam = (jnp.exp(jnp.sum(lam4[0:1] * lam4[1:2], axis=-1, keepdims=True))
               - jnp.exp(jnp.sum(lam4[2:3] * lam4[3:4], axis=-1, keepdims=True)) + lambda_init)
        o = o12[:tq] - lam * o12[tq:]
        o = _rms(o, subln_ref[...]) * (1.0 - lambda_init)
        o_ref[...] = o.astype(BF16)


def _diff_attention(q, k, v, lam4, subln, bsz, seq_len, lambda_init):
    n = q.shape[0]
    tq = min(DIFF_TQ, seq_len)
    tk = min(DIFF_TK, seq_len)
    nq = seq_len // tq
    nk = seq_len // tk
    return pl.pallas_call(
        functools.partial(_diff_body, lambda_init=lambda_init),
        grid=(bsz, DIFF_HEADS, nq, nk),
        in_specs=[
            pl.BlockSpec((tq, LANES), lambda b, h, qi, ki: (b * nq + qi, h)),
            pl.BlockSpec((tk, LANES), lambda b, h, qi, ki: (b * nk + ki, h)),
            pl.BlockSpec((tk, LANES), lambda b, h, qi, ki: (b * nk + ki, h)),
            pl.BlockSpec((4, HEAD_DIM), lambda b, h, qi, ki: (0, 0)),
            pl.BlockSpec((1, LANES), lambda b, h, qi, ki: (0, 0)),
        ],
        out_specs=pl.BlockSpec((tq, LANES), lambda b, h, qi, ki: (b * nq + qi, h)),
        out_shape=jax.ShapeDtypeStruct((n, D_MODEL), BF16),
        scratch_shapes=[pltpu.VMEM((2 * tq, 1), F32), pltpu.VMEM((2 * tq, 1), F32),
                        pltpu.VMEM((2 * tq, LANES), F32)],
        compiler_params=_cparams(("parallel", "parallel", "parallel", "arbitrary")),
        name="diff_attention",
    )(q, k, v, lam4, subln)


def _oproj_body(o_ref, x_ref, wo_ref, g_ref, wr_ref, xn_ref, aff_ref):
    xn = x_ref[...] + _dot(o_ref[...], wo_ref[...])
    xn_ref[...] = xn
    h = _rms(xn, g_ref[...]).astype(BF16)
    logits = _dot_nt(wr_ref[...], h)
    m = jnp.max(logits, axis=0, keepdims=True)
    e = jnp.exp(logits - m)
    aff_ref[...] = e / jnp.sum(e, axis=0, keepdims=True)


def _oproj_router(o, x, wo, g, wr_t):
    n = x.shape[0]
    tm = PROJ_TM
    row = lambda i: (i, 0)
    const = lambda i: (0, 0)
    return pl.pallas_call(
        _oproj_body,
        grid=(n // tm,),
        in_specs=[pl.BlockSpec((tm, D_MODEL), row), pl.BlockSpec((tm, D_MODEL), row),
                  pl.BlockSpec((D_MODEL, D_MODEL), const), pl.BlockSpec((1, D_MODEL), const),
                  pl.BlockSpec((N_EXPERTS, D_MODEL), const)],
        out_specs=[pl.BlockSpec((tm, D_MODEL), row), pl.BlockSpec((N_EXPERTS, tm), lambda i: (0, i))],
        out_shape=[jax.ShapeDtypeStruct((n, D_MODEL), F32),
                   jax.ShapeDtypeStruct((N_EXPERTS, n), F32)],
        compiler_params=_cparams(("parallel",)),
        name="oproj_router",
    )(o, x, wo, g, wr_t)


def _route_body(a_ref, idx_ref, gate_ref, pos_ref, off_ref, *, cap):
    e_id = pl.program_id(0)
    a = a_ref[0]
    nr = a.shape[0]
    bits = pltpu.bitcast(a, jnp.int32)

    def count(mask):
        c = jnp.sum(mask.astype(F32), axis=1, keepdims=True)
        return jnp.sum(c, axis=0, keepdims=True)

    def search(step, thr):
        cand = thr | lax.shift_left(jnp.int32(1), 30 - step)
        return jnp.where(count(bits >= cand) >= cap, cand, thr)

    thr = lax.fori_loop(0, 31, search, jnp.zeros((1, 1), jnp.int32))
    gt = bits > thr
    eq = bits == thr
    need = cap - count(gt)

    li = lax.broadcasted_iota(jnp.int32, (LANES, LANES), 0)
    lj = lax.broadcasted_iota(jnp.int32, (LANES, LANES), 1)
    upper_incl = (li <= lj).astype(BF16)
    ri = lax.broadcasted_iota(jnp.int32, (nr, nr), 0)
    rj = lax.broadcasted_iota(jnp.int32, (nr, nr), 1)
    lower_strict = (rj < ri).astype(BF16)

    def prefix(mask):
        w = _dot(mask.astype(BF16), upper_incl)
        tot = jnp.broadcast_to(w[:, LANES - 1:LANES], (nr, LANES))
        off = _dot(lower_strict, tot.astype(BF16))
        return w, tot, off

    w_eq, _, off_eq = prefix(eq)
    sel = gt | (eq & (off_eq + w_eq <= need))
    w, tot, off = prefix(sel)
    base = (e_id * cap).astype(F32)
    pos_ref[0] = jnp.where(sel, base + off + w - 1.0, -1.0)
    off_ref[0] = off

    off_excl = off[:, 0:1]
    off_incl = off_excl + tot[:, 0:1]
    w_t = w.T.astype(BF16)
    a_t = a.T
    a_hi_t = a_t.astype(BF16)
    r1 = a_t - a_hi_t.astype(F32)
    a_mid_t = r1.astype(BF16)
    a_lo_t = (r1 - a_mid_t.astype(F32)).astype(BF16)
    r_col = lax.broadcasted_iota(jnp.int32, (nr, LANES), 0).astype(F32)
    l_col = lax.broadcasted_iota(jnp.int32, (LANES, LANES), 0).astype(F32)

    def slot_tile(t, carry):
        j = (t * LANES + lax.broadcasted_iota(jnp.int32, (1, LANES), 1)).astype(F32)
        hit = (off_excl <= j) & (j < off_incl)
        hit_b = hit.astype(BF16)
        k = j - jnp.sum(jnp.where(hit, off_excl, 0.0), axis=0, keepdims=True)
        row = jnp.sum(jnp.where(hit, r_col, 0.0), axis=0, keepdims=True)
        wsel = _dot(w_t, hit_b)
        lane = jnp.sum((wsel <= k).astype(F32), axis=0, keepdims=True)
        asel = _dot(a_hi_t, hit_b) + _dot(a_mid_t, hit_b) + _dot(a_lo_t, hit_b)
        gate = jnp.sum(jnp.where(l_col == lane, asel, 0.0), axis=0, keepdims=True)
        idx_ref[0, pl.ds(t, 1), :] = (row * LANES + lane).astype(jnp.int32)
        gate_ref[0, pl.ds(t, 1), :] = gate
        return carry

    lax.fori_loop(0, cap // LANES, slot_tile, 0)


def _route(aff_t, cap):
    n = aff_t.shape[1]
    nr = n // LANES
    a3 = aff_t.reshape(N_EXPERTS, nr, LANES)
    blk = lambda r: pl.BlockSpec((1, r, LANES), lambda e: (e, 0, 0))
    return pl.pallas_call(
        functools.partial(_route_body, cap=cap),
        grid=(N_EXPERTS,),
        in_specs=[blk(nr)],
        out_specs=[blk(cap // LANES), blk(cap // LANES), blk(nr), blk(nr)],
        out_shape=[jax.ShapeDtypeStruct((N_EXPERTS, cap // LANES, LANES), jnp.int32),
                   jax.ShapeDtypeStruct((N_EXPERTS, cap // LANES, LANES), F32),
                   jax.ShapeDtypeStruct((N_EXPERTS, nr, LANES), F32),
                   jax.ShapeDtypeStruct((N_EXPERTS, nr, LANES), F32)],
        compiler_params=_cparams(("parallel",)),
        name="ec_route",
    )(a3)


FFN_TC = 512
FFN_FCHUNK = 512


def _ffn_body(idx_cur, idx_nxt, x_hbm, g_ref, gate_ref, wg_ref, wu_ref, wd_ref, y_ref, xbuf, sem):
    tc = xbuf.shape[1]
    step = pl.program_id(0) * pl.num_programs(1) + pl.program_id(1)
    total = pl.num_programs(0) * pl.num_programs(1)
    slot = step % 2

    def gather(idx, dst_slot):
        def one(r, c):
            pltpu.make_async_copy(x_hbm.at[pl.ds(idx[0, 0, r], 1)],
                                  xbuf.at[dst_slot, pl.ds(r, 1)], sem.at[dst_slot]).start()
            return c
        lax.fori_loop(0, tc, one, 0)

    @pl.when(step == 0)
    def _():
        gather(idx_cur, 0)

    @pl.when(step + 1 < total)
    def _():
        gather(idx_nxt, 1 - slot)

    pltpu.make_async_copy(x_hbm.at[pl.ds(0, tc)], xbuf.at[slot], sem.at[slot]).wait()

    xe = _rms(xbuf[slot], g_ref[...]).astype(BF16)
    acc = jnp.zeros((tc, D_MODEL), F32)
    for c in range(D_EXPERT // FFN_FCHUNK):
        fs = slice(c * FFN_FCHUNK, (c + 1) * FFN_FCHUNK)
        hg = _dot(xe, wg_ref[0, :, fs])
        hu = _dot(xe, wu_ref[0, :, fs])
        hid = (hg * jax.nn.sigmoid(hg)) * hu
        acc = acc + _dot(hid.astype(BF16), wd_ref[0, fs, :])
    y_ref[...] = (acc * gate_ref[0]).astype(BF16)


def _expert_ffn(x, g, idx, gate, wg, wu, wd, cap):
    tc = min(FFN_TC, cap)
    nt = cap // tc
    ntiles = N_EXPERTS * nt
    idx3 = idx.reshape(ntiles, 1, tc)
    gate3 = gate.reshape(N_EXPERTS, cap, 1)
    smem = lambda m: pl.BlockSpec((1, 1, tc), m, memory_space=pltpu.SMEM)
    wspec = lambda a, b: pl.BlockSpec((1, a, b), lambda e, i: (e, 0, 0))
    return pl.pallas_call(
        _ffn_body,
        grid=(N_EXPERTS, nt),
        in_specs=[smem(lambda e, i: (e * nt + i, 0, 0)),
                  smem(lambda e, i: (jnp.minimum(e * nt + i + 1, ntiles - 1), 0, 0)),
                  pl.BlockSpec(memory_space=pl.ANY),
                  pl.BlockSpec((1, D_MODEL), lambda e, i: (0, 0)),
                  pl.BlockSpec((1, tc, 1), lambda e, i: (e, i, 0)),
                  wspec(D_MODEL, D_EXPERT), wspec(D_MODEL, D_EXPERT), wspec(D_EXPERT, D_MODEL)],
        out_specs=pl.BlockSpec((tc, D_MODEL), lambda e, i: (e * nt + i, 0)),
        out_shape=jax.ShapeDtypeStruct((N_EXPERTS * cap, D_MODEL), BF16),
        scratch_shapes=[pltpu.VMEM((2, tc, D_MODEL), F32), pltpu.SemaphoreType.DMA((2,))],
        compiler_params=_cparams(("arbitrary", "arbitrary")),
        name="expert_ffn",
    )(idx3, idx3, x, g, gate3, wg, wu, wd)


COMB_TB = 512
COMB_CW = 128
COMB_ALIGN = 16


def _combine_body(boff_ref, x_ref, pos_ref, y_hbm, gfin_ref, o_ref, win, xwin, sem, xsem,
                  *, cap, final_norm):
    b = pl.program_id(0)
    total = N_EXPERTS * cap
    tb = x_ref.shape[0]
    cw = COMB_CW

    def wstart(e):
        s = jnp.bitwise_and(e * cap + boff_ref[e, b], -COMB_ALIGN)
        return pl.multiple_of(jnp.minimum(s, total - cw), COMB_ALIGN)

    starts = [wstart(e) for e in range(N_EXPERTS)]
    copies = [pltpu.make_async_copy(y_hbm.at[pl.ds(starts[e], cw)], win.at[pl.ds(e * cw, cw)],
                                    sem.at[e]) for e in range(N_EXPERTS)]
    for c in copies:
        c.start()
    lane = lax.broadcasted_iota(jnp.int32, (tb, cw), 1).astype(F32)
    pos = pos_ref[...]
    onehot = jnp.concatenate(
        [(pos[:, e:e + 1] - starts[e].astype(F32) == lane).astype(BF16) for e in range(N_EXPERTS)],
        axis=1)
    for c in copies:
        c.wait()
    o_ref[...] = x_ref[...] + _dot(onehot, win[...])

    for e in range(N_EXPERTS):
        end = e * cap + boff_ref[e, b + 1]
        extra = lax.shift_right_logical(jnp.maximum(end - starts[e] - 1, 0), int(math.log2(cw)))

        def more(c, carry, e=e):
            lo = starts[e] + c * cw
            s = pl.multiple_of(jnp.minimum(lo, total - cw), COMB_ALIGN)
            cp = pltpu.make_async_copy(y_hbm.at[pl.ds(s, cw)], xwin, xsem.at[0])
            cp.start()
            cp.wait()
            pe = pos[:, e:e + 1]
            oh = ((pe - s.astype(F32) == lane) & (pe >= lo.astype(F32))).astype(BF16)
            o_ref[...] += _dot(oh, xwin[...])
            return carry

        lax.fori_loop(1, extra + 1, more, 0)

    if final_norm:
        o_ref[...] = _rms(o_ref[...], gfin_ref[...])


def _combine(x, pos_t, boff, y, gfin, cap, final_norm):
    n = x.shape[0]
    tb = COMB_TB
    grid_spec = pltpu.PrefetchScalarGridSpec(
        num_scalar_prefetch=1,
        grid=(n // tb,),
        in_specs=[pl.BlockSpec((tb, D_MODEL), lambda b, s: (b, 0)),
                  pl.BlockSpec((tb, N_EXPERTS), lambda b, s: (b, 0)),
                  pl.BlockSpec(memory_space=pl.ANY),
                  pl.BlockSpec((1, D_MODEL), lambda b, s: (0, 0))],
        out_specs=pl.BlockSpec((tb, D_MODEL), lambda b, s: (b, 0)),
        scratch_shapes=[pltpu.VMEM((N_EXPERTS * COMB_CW, D_MODEL), BF16),
                        pltpu.VMEM((COMB_CW, D_MODEL), BF16),
                        pltpu.SemaphoreType.DMA((N_EXPERTS,)),
                        pltpu.SemaphoreType.DMA((1,))],
    )
    return pl.pallas_call(
        functools.partial(_combine_body, cap=cap, final_norm=final_norm),
        grid_spec=grid_spec,
        out_shape=jax.ShapeDtypeStruct((n, D_MODEL), F32),
        compiler_params=_cparams(("arbitrary",)),
        name="ec_combine",
    )(boff, x, pos_t, y, gfin)


def _moe(xn, aff_t, ffn_norm, wg, wu, wd, gfin, final_norm):
    n = xn.shape[0]
    cap = EC_CAPACITY_FACTOR * n // N_EXPERTS
    idx, gate, pos, off = _route(aff_t, cap)
    y = _expert_ffn(xn, ffn_norm, idx, gate, wg, wu, wd, cap)
    pos_t = pos.reshape(N_EXPERTS, n).T
    rows_per_blk = COMB_TB // LANES
    boff = off[:, ::rows_per_blk, 0].astype(jnp.int32)
    boff = jnp.concatenate([boff, jnp.full((N_EXPERTS, 1), cap, jnp.int32)], axis=1)
    return _combine(xn, pos_t, boff, y, gfin, cap, final_norm)


def _expand_kv_columns(w_qkv):
    qd = SW_HEADS * HEAD_DIM
    kd = SW_KV_HEADS * HEAD_DIM
    head_of = jnp.arange(SW_HEADS) // (SW_HEADS // SW_KV_HEADS)
    cols = (head_of[:, None] * HEAD_DIM + jnp.arange(HEAD_DIM)[None, :]).reshape(-1)
    return jnp.concatenate([w_qkv[:, :qd], w_qkv[:, qd:qd + kd][:, cols],
                            w_qkv[:, qd + kd:][:, cols]], axis=1)


def _run_group(x3, layers, final_norm_g):
    bsz, seq_len, _ = x3.shape
    n = bsz * seq_len
    x = x3.reshape(n, D_MODEL)
    cos_t, sin_t = _rope_tables(seq_len)
    gfin = final_norm_g.reshape(1, D_MODEL).astype(F32)
    for i, lp in enumerate(layers):
        kind = MIXER_PATTERN[i % len(MIXER_PATTERN)]
        q, k, v = _qkv_proj(x, lp["mix_norm"], lp["w_qkv"], cos_t, sin_t, seq_len, kind != "nat")
        if kind == "nat":
            o = _nat_attention(q, k, v, lp["rpb"], bsz, seq_len)
        elif kind == "swa":
            o = _swa_attention(q, k, v, lp["sink"], bsz, seq_len)
        else:
            lambda_init = 0.8 - 0.6 * math.exp(-0.3 * i)
            o = _diff_attention(q, k, v, lp["lam4"], lp["subln"], bsz, seq_len, lambda_init)
        xn, aff_t = _oproj_router(o, x, lp["w_o"], lp["ffn_norm"], lp["w_router_t"])
        x = _moe(xn, aff_t, lp["ffn_norm"], lp["w_gate"], lp["w_up"], lp["w_down"],
                 gfin, i == len(layers) - 1)
    return x.reshape(bsz, seq_len, D_MODEL)


def _prepare_layer(i, lp):
    kind = MIXER_PATTERN[i % len(MIXER_PATTERN)]
    out = dict(
        mix_norm=lp["mix_norm"].reshape(1, D_MODEL).astype(F32),
        ffn_norm=lp["ffn_norm"].reshape(1, D_MODEL).astype(F32),
        w_o=lp["w_o"].astype(BF16),
        w_router_t=lp["w_router"].T.astype(BF16),
        w_gate=lp["w_gate"].astype(BF16), w_up=lp["w_up"].astype(BF16),
        w_down=lp["w_down"].astype(BF16),
    )
    if kind == "swa":
        out["w_qkv"] = _expand_kv_columns(lp["w_qkv"]).astype(BF16)
        out["sink"] = lp["sink"]
    else:
        out["w_qkv"] = lp["w_qkv"].astype(BF16)
    if kind == "nat":
        out["rpb"] = lp["rpb"]
    if kind == "diff":
        out["lam4"] = jnp.stack([lp["lambda_q1"], lp["lambda_k1"], lp["lambda_q2"],
                                 lp["lambda_k2"]]).astype(F32)
        out["subln"] = lp["subln"].reshape(1, LANES).astype(F32)
    return out


def kernel(x_prompt, x_sample, l0_mix_norm, l0_w_qkv, l0_rpb, l0_w_o, l0_ffn_norm, l0_w_router, l0_w_gate, l0_w_up, l0_w_down, l1_mix_norm, l1_w_qkv, l1_sink, l1_w_o, l1_ffn_norm, l1_w_router, l1_w_gate, l1_w_up, l1_w_down, l2_mix_norm, l2_w_qkv, l2_lambda_q1, l2_lambda_k1, l2_lambda_q2, l2_lambda_k2, l2_subln, l2_w_o, l2_ffn_norm, l2_w_router, l2_w_gate, l2_w_up, l2_w_down, l3_mix_norm, l3_w_qkv, l3_rpb, l3_w_o, l3_ffn_norm, l3_w_router, l3_w_gate, l3_w_up, l3_w_down, final_norm):
    layers = [
        dict(mix_norm=l0_mix_norm, w_qkv=l0_w_qkv, rpb=l0_rpb, w_o=l0_w_o, ffn_norm=l0_ffn_norm,
             w_router=l0_w_router, w_gate=l0_w_gate, w_up=l0_w_up, w_down=l0_w_down),
        dict(mix_norm=l1_mix_norm, w_qkv=l1_w_qkv, sink=l1_sink, w_o=l1_w_o, ffn_norm=l1_ffn_norm,
             w_router=l1_w_router, w_gate=l1_w_gate, w_up=l1_w_up, w_down=l1_w_down),
        dict(mix_norm=l2_mix_norm, w_qkv=l2_w_qkv, lambda_q1=l2_lambda_q1, lambda_k1=l2_lambda_k1,
             lambda_q2=l2_lambda_q2, lambda_k2=l2_lambda_k2, subln=l2_subln, w_o=l2_w_o,
             ffn_norm=l2_ffn_norm, w_router=l2_w_router, w_gate=l2_w_gate, w_up=l2_w_up,
             w_down=l2_w_down),
        dict(mix_norm=l3_mix_norm, w_qkv=l3_w_qkv, rpb=l3_rpb, w_o=l3_w_o, ffn_norm=l3_ffn_norm,
             w_router=l3_w_router, w_gate=l3_w_gate, w_up=l3_w_up, w_down=l3_w_down),
    ]
    layers = [_prepare_layer(i, lp) for i, lp in enumerate(layers)]
    y_prompt = _run_group(x_prompt, layers, final_norm)
    y_sample = _run_group(x_sample, layers, final_norm)
    return (y_prompt, y_sample)
```

```python
import functools
import math

import jax
import jax.numpy as jnp
import numpy as np
from jax import lax
from jax.experimental import pallas as pl
from jax.experimental.pallas import tpu as pltpu

D_MODEL = 1024
DEPTH = 4
GRID_W = 64
NA_HEADS = 16
NA_WIN_H = 8
NA_WIN_W = 16
SW_HEADS = 16
SW_KV_HEADS = 4
SW_BLOCK = 128
DIFF_HEADS = 8
HEAD_DIM = 64
N_EXPERTS = 16
EC_CAPACITY_FACTOR = 2
D_EXPERT = 2048
ROPE_THETA = 10000.0
NORM_EPS = 1e-6
MIXER_PATTERN = ("nat", "swa", "diff")

LANES = 128
VMEM_LIMIT = 56 * 1024 * 1024

BF16 = jnp.bfloat16
F32 = jnp.float32
NEG_INF = float("-inf")


def _cparams(sem):
    return pltpu.CompilerParams(dimension_semantics=sem, vmem_limit_bytes=VMEM_LIMIT)


def _dot(a, b):
    return jnp.dot(a, b, preferred_element_type=F32)


def _dot_nt(a, b):
    return lax.dot_general(a, b, (((1,), (1,)), ((), ())), preferred_element_type=F32)


def _rms(x, g):
    return x * lax.rsqrt(jnp.mean(x * x, axis=-1, keepdims=True) + NORM_EPS) * g


PROJ_TM = 256


def _qkv_body(x_ref, g_ref, w_ref, cos_ref, sin_ref, q_ref, k_ref, v_ref, *, use_rope):
    h = _rms(x_ref[...], g_ref[...])
    y = _dot(h.astype(BF16), w_ref[...])
    tm = y.shape[0]
    lane = lax.broadcasted_iota(jnp.int32, (tm, LANES), 1)
    first_half = (lane % HEAD_DIM) < (HEAD_DIM // 2)
    if use_rope:
        cos = cos_ref[...]
        sin = sin_ref[...]
    outs = (q_ref, k_ref, v_ref)
    for part in range(3):
        for c in range(D_MODEL // LANES):
            col = part * D_MODEL + c * LANES
            blk = y[:, col:col + LANES]
            if use_rope and part < 2:
                rot = jnp.where(first_half,
                                pltpu.roll(blk, LANES - HEAD_DIM // 2, 1),
                                pltpu.roll(blk, HEAD_DIM // 2, 1))
                blk = blk * cos + rot * sin
            if part == 0:
                blk = blk * (HEAD_DIM ** -0.5)
            outs[part][:, c * LANES:(c + 1) * LANES] = blk.astype(BF16)


def _qkv_proj(x, g, w, cos, sin, seq_len, use_rope):
    n = x.shape[0]
    tm = PROJ_TM
    tblocks = seq_len // tm
    out = jax.ShapeDtypeStruct((n, D_MODEL), BF16)
    row = lambda i: (i, 0)
    return pl.pallas_call(
        functools.partial(_qkv_body, use_rope=use_rope),
        grid=(n // tm,),
        in_specs=[
            pl.BlockSpec((tm, D_MODEL), row),
            pl.BlockSpec((1, D_MODEL), lambda i: (0, 0)),
            pl.BlockSpec((D_MODEL, 3 * D_MODEL), lambda i: (0, 0)),
            pl.BlockSpec((tm, LANES), lambda i: (i % tblocks, 0)),
            pl.BlockSpec((tm, LANES), lambda i: (i % tblocks, 0)),
        ],
        out_specs=[pl.BlockSpec((tm, D_MODEL), row)] * 3,
        out_shape=[out, out, out],
        compiler_params=_cparams(("parallel",)),
        name="qkv_proj",
    )(x, g, w, cos, sin)


def _rope_tables(seq_len):
    inv = ROPE_THETA ** (-jnp.arange(0, HEAD_DIM, 2, dtype=F32) / HEAD_DIM)
    ang = jnp.arange(seq_len, dtype=F32)[:, None] * inv[None, :]
    cos = jnp.cos(ang)
    sin = jnp.sin(ang)
    cos_t = jnp.tile(jnp.concatenate([cos, cos], axis=1), (1, LANES // HEAD_DIM))
    sin_t = jnp.tile(jnp.concatenate([-sin, sin], axis=1), (1, LANES // HEAD_DIM))
    return cos_t, sin_t


def _pair_split(qp):
    lane = lax.broadcasted_iota(jnp.int32, qp.shape, 1)
    zero = jnp.zeros_like(qp)
    return jnp.concatenate([jnp.where(lane < HEAD_DIM, qp, zero),
                            jnp.where(lane >= HEAD_DIM, qp, zero)], axis=0)


def _pair_merge(o, m):
    lane = lax.broadcasted_iota(jnp.int32, (m, LANES), 1)
    return jnp.where(lane < HEAD_DIM, o[:m], o[m:])


NAT_ROWS_PER_STEP = 4
NAT_TQ = NAT_ROWS_PER_STEP * GRID_W
NAT_WIN_BLOCKS = 3
NAT_PAIRS_PER_STEP = 2


def _nat_body(q_ref, k0_ref, k1_ref, k2_ref, v0_ref, v1_ref, v2_ref, b_ref, o_ref):
    krefs = (k0_ref, k1_ref, k2_ref)
    vrefs = (v0_ref, v1_ref, v2_ref)
    for p in range(NAT_PAIRS_PER_STEP):
        cs = slice(p * LANES, (p + 1) * LANES)
        qs = _pair_split(q_ref[:, cs])
        s = [_dot_nt(qs, krefs[j][:, cs]) + b_ref[0, p, :, j * NAT_TQ:(j + 1) * NAT_TQ]
             for j in range(NAT_WIN_BLOCKS)]
        m = jnp.maximum(jnp.maximum(jnp.max(s[0], axis=-1, keepdims=True),
                                    jnp.max(s[1], axis=-1, keepdims=True)),
                        jnp.max(s[2], axis=-1, keepdims=True))
        e = [jnp.exp(sj - m) for sj in s]
        l = (jnp.sum(e[0], axis=-1, keepdims=True) + jnp.sum(e[1], axis=-1, keepdims=True)
             + jnp.sum(e[2], axis=-1, keepdims=True))
        r = 1.0 / l
        o = (_dot((e[0] * r).astype(BF16), vrefs[0][:, cs])
             + _dot((e[1] * r).astype(BF16), vrefs[1][:, cs])
             + _dot((e[2] * r).astype(BF16), vrefs[2][:, cs]))
        o_ref[:, cs] = _pair_merge(o, NAT_TQ).astype(BF16)


def _nat_bias_table(rpb):
    cols = jnp.arange(GRID_W)
    col_start = jnp.clip(cols - NA_WIN_W // 2, 0, GRID_W - NA_WIN_W)
    col_in = (cols[None, :] >= col_start[:, None]) & (cols[None, :] < col_start[:, None] + NA_WIN_W)
    dc_idx = jnp.clip(cols[None, :] - cols[:, None] + NA_WIN_W - 1, 0, 2 * NA_WIN_W - 2)
    tz = jnp.where(col_in[None, None], rpb.astype(F32)[:, :, dc_idx], NEG_INF)
    n_dr = 2 * NA_WIN_H - 1
    tz = jnp.concatenate([tz, jnp.full((NA_HEADS, 1, GRID_W, GRID_W), NEG_INF, F32)], axis=1)
    win_rows = NAT_WIN_BLOCKS * NAT_ROWS_PER_STEP
    half = NA_WIN_H // 2
    d_idx = np.full((3, NAT_ROWS_PER_STEP, win_rows), n_dr, np.int32)
    for case in range(3):
        for a in range(NAT_ROWS_PER_STEP):
            j0, d0 = ((0, NA_WIN_H - 1 - a), (a, NA_WIN_H - 1 - half), (half, NA_WIN_H - 1 - half - a))[case]
            for jj in range(NA_WIN_H):
                d_idx[case, a, j0 + jj] = d0 + jj
    t = tz[:, d_idx]
    t = jnp.transpose(t, (1, 0, 2, 4, 3, 5))
    return t.reshape(3, NA_HEADS // 2, 2 * NAT_TQ, win_rows * GRID_W)


def _nat_attention(q, k, v, bias, bsz, seq_len):
    n = q.shape[0]
    rows = seq_len // GRID_W
    nb = rows // NAT_ROWS_PER_STEP
    assert rows % NAT_ROWS_PER_STEP == 0 and nb >= NAT_WIN_BLOCKS
    pp = NAT_PAIRS_PER_STEP
    width = pp * LANES

    def win(j):
        return lambda hp, b, i: (b * nb + jnp.clip(i - 1, 0, nb - NAT_WIN_BLOCKS) + j, hp)

    def bias_map(hp, b, i):
        case = jnp.where(i == 0, 0, jnp.where(i == nb - 1, 2, 1))
        return (case, hp, 0, 0)

    qmap = lambda hp, b, i: (b * nb + i, hp)
    blk = lambda m: pl.BlockSpec((NAT_TQ, width), m)
    return pl.pallas_call(
        _nat_body,
        grid=(NA_HEADS // 2 // pp, bsz, nb),
        in_specs=[blk(qmap), blk(win(0)), blk(win(1)), blk(win(2)),
                  blk(win(0)), blk(win(1)), blk(win(2)),
                  pl.BlockSpec((1, pp, 2 * NAT_TQ, NAT_WIN_BLOCKS * NAT_TQ), bias_map)],
        out_specs=blk(qmap),
        out_shape=jax.ShapeDtypeStruct((n, D_MODEL), BF16),
        compiler_params=_cparams(("parallel", "parallel", "parallel")),
        name="nat_attention",
    )(q, k, k, k, v, v, v, bias)


def _swa_body(sink_ref, q_ref, k0_ref, k1_ref, k2_ref, v0_ref, v1_ref, v2_ref, o_ref, *, nblk):
    i = pl.program_id(1)
    tb = SW_BLOCK
    qq = lax.broadcasted_iota(jnp.int32, (2 * tb, tb), 0) % tb
    kk = lax.broadcasted_iota(jnp.int32, (2 * tb, tb), 1)
    valid_prev = (kk >= qq) & (i > 0)
    valid_next = (kk <= qq) & (i < nblk - 1)
    row = lax.broadcasted_iota(jnp.int32, (2 * tb, 1), 0)
    for w in range(SW_HEADS // 2):
        cs = slice(w * LANES, (w + 1) * LANES)
        qs = _pair_split(q_ref[:, cs])
        s0 = jnp.where(valid_prev, _dot_nt(qs, k0_ref[:, cs]), NEG_INF)
        s1 = _dot_nt(qs, k1_ref[:, cs])
        s2 = jnp.where(valid_next, _dot_nt(qs, k2_ref[:, cs]), NEG_INF)
        sink = jnp.where(row < tb, sink_ref[2 * w], sink_ref[2 * w + 1])
        m = jnp.maximum(jnp.maximum(jnp.max(s0, axis=-1, keepdims=True),
                                    jnp.max(s1, axis=-1, keepdims=True)),
                        jnp.maximum(jnp.max(s2, axis=-1, keepdims=True), sink))
        e0 = jnp.exp(s0 - m)
        e1 = jnp.exp(s1 - m)
        e2 = jnp.exp(s2 - m)
        l = (jnp.sum(e0, axis=-1, keepdims=True) + jnp.sum(e1, axis=-1, keepdims=True)
             + jnp.sum(e2, axis=-1, keepdims=True) + jnp.exp(sink - m))
        o = (_dot((e0 / l).astype(BF16), v0_ref[:, cs]) + _dot((e1 / l).astype(BF16), v1_ref[:, cs])
             + _dot((e2 / l).astype(BF16), v2_ref[:, cs]))
        o_ref[:, cs] = _pair_merge(o, tb).astype(BF16)


def _swa_attention(q, k, v, sink, bsz, seq_len):
    n = q.shape[0]
    nblk = seq_len // SW_BLOCK
    cur = lambda b, i: (b * nblk + i, 0)
    prev = lambda b, i: (b * nblk + jnp.maximum(i - 1, 0), 0)
    nxt = lambda b, i: (b * nblk + jnp.minimum(i + 1, nblk - 1), 0)
    blk = lambda m: pl.BlockSpec((SW_BLOCK, D_MODEL), m)
    return pl.pallas_call(
        functools.partial(_swa_body, nblk=nblk),
        grid=(bsz, nblk),
        in_specs=[pl.BlockSpec(memory_space=pltpu.SMEM),
                  blk(cur), blk(prev), blk(cur), blk(nxt), blk(prev), blk(cur), blk(nxt)],
        out_specs=blk(cur),
        out_shape=jax.ShapeDtypeStruct((n, D_MODEL), BF16),
        compiler_params=_cparams(("parallel", "parallel")),
        name="swa_attention",
    )(sink.astype(F32), q, k, k, k, v, v, v)


DIFF_TQ = 1024
DIFF_TK = 512


def _diff_body(q_ref, k_ref, v_ref, lam_ref, subln_ref, o_ref, qs_sc, m_sc, l_sc, acc_sc,
               *, lambda_init):
    kv = pl.program_id(3)
    tq = q_ref.shape[0]
    tk = k_ref.shape[0]

    @pl.when(kv == 0)
    def _():
        qs_sc[...] = _pair_split(q_ref[...])
        m_sc[...] = jnp.full(m_sc.shape, NEG_INF, F32)
        l_sc[...] = jnp.zeros(l_sc.shape, F32)
        acc_sc[...] = jnp.zeros(acc_sc.shape, F32)

    s = _dot_nt(qs_sc[...], k_ref[...])
    m_prev = m_sc[...]
    m_new = jnp.maximum(m_prev, jnp.max(s, axis=-1, keepdims=True))
    alpha = jnp.exp(m_prev - m_new)
    p = jnp.exp(s - jnp.concatenate([m_new] * (tk // LANES), axis=1))
    l_sc[...] = alpha * l_sc[...] + jnp.sum(p, axis=-1, keepdims=True)
    acc_sc[...] = alpha * acc_sc[...] + _dot(p.astype(BF16), v_ref[...])
    m_sc[...] = m_new

    @pl.when(kv == pl.num_programs(3) - 1)
    def _():
        o12 = acc_sc[...] / l_sc[...]
        lam4 = lam_ref[...]
        lam = (jnp.exp(jnp.sum(lam4[0:1] * lam4[1:2], axis=-1, keepdims=True))
               - jnp.exp(jnp.sum(lam4[2:3] * lam4[3:4], axis=-1, keepdims=True)) + lambda_init)
        o = o12[:tq] - lam * o12[tq:]
        o = _rms(o, subln_ref[...]) * (1.0 - lambda_init)
        o_ref[...] = o.astype(BF16)


def _diff_attention(q, k, v, lam4, subln, bsz, seq_len, lambda_init):
    n = q.shape[0]
    tq = min(DIFF_TQ, seq_len)
    tk = min(DIFF_TK, seq_len)
    nq = seq_len // tq
    nk = seq_len // tk
    return pl.pallas_call(
        functools.partial(_diff_body, lambda_init=lambda_init),
        grid=(bsz, DIFF_HEADS, nq, nk),
        in_specs=[
            pl.BlockSpec((tq, LANES), lambda b, h, qi, ki: (b * nq + qi, h)),
            pl.BlockSpec((tk, LANES), lambda b, h, qi, ki: (b * nk + ki, h)),
            pl.BlockSpec((tk, LANES), lambda b, h, qi, ki: (b * nk + ki, h)),
            pl.BlockSpec((4, HEAD_DIM), lambda b, h, qi, ki: (0, 0)),
            pl.BlockSpec((1, LANES), lambda b, h, qi, ki: (0, 0)),
        ],
        out_specs=pl.BlockSpec((tq, LANES), lambda b, h, qi, ki: (b * nq + qi, h)),
        out_shape=jax.ShapeDtypeStruct((n, D_MODEL), BF16),
        scratch_shapes=[pltpu.VMEM((2 * tq, LANES), BF16), pltpu.VMEM((2 * tq, LANES), F32),
                        pltpu.VMEM((2 * tq, LANES), F32), pltpu.VMEM((2 * tq, LANES), F32)],
        compiler_params=_cparams(("parallel", "parallel", "parallel", "arbitrary")),
        name="diff_attention",
    )(q, k, v, lam4, subln)


def _oproj_body(o_ref, x_ref, wo_ref, g_ref, wr_ref, xn_ref, aff_ref):
    xn = x_ref[...] + _dot(o_ref[...], wo_ref[...])
    xn_ref[...] = xn
    h = _rms(xn, g_ref[...]).astype(BF16)
    logits = _dot_nt(wr_ref[...], h)
    m = jnp.max(logits, axis=0, keepdims=True)
    e = jnp.exp(logits - m)
    aff_ref[...] = e / jnp.sum(e, axis=0, keepdims=True)


def _oproj_router(o, x, wo, g, wr_t):
    n = x.shape[0]
    tm = PROJ_TM
    row = lambda i: (i, 0)
    const = lambda i: (0, 0)
    return pl.pallas_call(
        _oproj_body,
        grid=(n // tm,),
        in_specs=[pl.BlockSpec((tm, D_MODEL), row), pl.BlockSpec((tm, D_MODEL), row),
                  pl.BlockSpec((D_MODEL, D_MODEL), const), pl.BlockSpec((1, D_MODEL), const),
                  pl.BlockSpec((N_EXPERTS, D_MODEL), const)],
        out_specs=[pl.BlockSpec((tm, D_MODEL), row), pl.BlockSpec((N_EXPERTS, tm), lambda i: (0, i))],
        out_shape=[jax.ShapeDtypeStruct((n, D_MODEL), F32),
                   jax.ShapeDtypeStruct((N_EXPERTS, n), F32)],
        compiler_params=_cparams(("parallel",)),
        name="oproj_router",
    )(o, x, wo, g, wr_t)


def _route_body(a_ref, idx_ref, gate_ref, pos_ref, off_ref, *, cap):
    e_id = pl.program_id(0)
    a = a_ref[0]
    nr = a.shape[0]
    bits = pltpu.bitcast(a, jnp.int32)

    def count(mask):
        c = jnp.sum(mask.astype(F32), axis=1, keepdims=True)
        return jnp.sum(c, axis=0, keepdims=True)

    def search(step, thr):
        cand = thr | lax.shift_left(jnp.int32(1), 30 - step)
        return jnp.where(count(bits >= cand) >= cap, cand, thr)

    thr = lax.fori_loop(0, 31, search, jnp.zeros((1, 1), jnp.int32))
    gt = bits > thr
    eq = bits == thr
    need = cap - count(gt)

    li = lax.broadcasted_iota(jnp.int32, (LANES, LANES), 0)
    lj = lax.broadcasted_iota(jnp.int32, (LANES, LANES), 1)
    upper_incl = (li <= lj).astype(BF16)
    ri = lax.broadcasted_iota(jnp.int32, (nr, nr), 0)
    rj = lax.broadcasted_iota(jnp.int32, (nr, nr), 1)
    lower_strict = (rj < ri).astype(BF16)

    def prefix(mask):
        w = _dot(mask.astype(BF16), upper_incl)
        tot = jnp.broadcast_to(w[:, LANES - 1:LANES], (nr, LANES))
        off = _dot(lower_strict, tot.astype(BF16))
        return w, tot, off

    w_eq, _, off_eq = prefix(eq)
    sel = gt | (eq & (off_eq + w_eq <= need))
    w, tot, off = prefix(sel)
    base = (e_id * cap).astype(F32)
    pos_ref[0] = jnp.where(sel, base + off + w - 1.0, -1.0)
    off_ref[0] = off

    off_excl = off[:, 0:1]
    off_incl = off_excl + tot[:, 0:1]
    w_t = w.T.astype(BF16)
    a_t = a.T
    a_hi_t = a_t.astype(BF16)
    r1 = a_t - a_hi_t.astype(F32)
    a_mid_t = r1.astype(BF16)
    a_lo_t = (r1 - a_mid_t.astype(F32)).astype(BF16)
    r_col = lax.broadcasted_iota(jnp.int32, (nr, LANES), 0).astype(F32)
    l_col = lax.broadcasted_iota(jnp.int32, (LANES, LANES), 0).astype(F32)

    def slot_tile(t, carry):
        j = (t * LANES + lax.broadcasted_iota(jnp.int32, (1, LANES), 1)).astype(F32)
        hit = (off_excl <= j) & (j < off_incl)
        hit_b = hit.astype(BF16)
        k = j - jnp.sum(jnp.where(hit, off_excl, 0.0), axis=0, keepdims=True)
        row = jnp.sum(jnp.where(hit, r_col, 0.0), axis=0, keepdims=True)
        wsel = _dot(w_t, hit_b)
        lane = jnp.sum((wsel <= k).astype(F32), axis=0, keepdims=True)
        asel = _dot(a_hi_t, hit_b) + _dot(a_mid_t, hit_b) + _dot(a_lo_t, hit_b)
        gate = jnp.sum(jnp.where(l_col == lane, asel, 0.0), axis=0, keepdims=True)
        idx_ref[0, pl.ds(t, 1), :] = (row * LANES + lane).astype(jnp.int32)
        gate_ref[0, pl.ds(t, 1), :] = gate
        return carry

    lax.fori_loop(0, cap // LANES, slot_tile, 0)


def _route(aff_t, cap):
    n = aff_t.shape[1]
    nr = n // LANES
    a3 = aff_t.reshape(N_EXPERTS, nr, LANES)
    blk = lambda r: pl.BlockSpec((1, r, LANES), lambda e: (e, 0, 0))
    return pl.pallas_call(
        functools.partial(_route_body, cap=cap),
        grid=(N_EXPERTS,),
        in_specs=[blk(nr)],
        out_specs=[blk(cap // LANES), blk(cap // LANES), blk(nr), blk(nr)],
        out_shape=[jax.ShapeDtypeStruct((N_EXPERTS, cap // LANES, LANES), jnp.int32),
                   jax.ShapeDtypeStruct((N_EXPERTS, cap // LANES, LANES), F32),
                   jax.ShapeDtypeStruct((N_EXPERTS, nr, LANES), F32),
                   jax.ShapeDtypeStruct((N_EXPERTS, nr, LANES), F32)],
        compiler_params=_cparams(("parallel",)),
        name="ec_route",
    )(a3)


FFN_TC = 512
FFN_FCHUNK = 512


def _ffn_body(idx_cur, idx_nxt, x_hbm, g_ref, gate_ref, wg_ref, wu_ref, wd_ref, y_ref, xbuf, sem):
    tc = xbuf.shape[1]
    step = pl.program_id(0) * pl.num_programs(1) + pl.program_id(1)
    total = pl.num_programs(0) * pl.num_programs(1)
    slot = step % 2

    def row_copy(idx, r, dst_slot):
        return pltpu.make_async_copy(x_hbm.at[pl.ds(idx[0, 0, r], 1)],
                                     xbuf.at[dst_slot, pl.ds(r, 1)], sem.at[dst_slot])

    def wait_slot(s):
        pltpu.make_async_copy(x_hbm.at[pl.ds(0, tc)], xbuf.at[s], sem.at[s]).wait()

    @pl.when(step == 0)
    def _():
        def one(r, c):
            row_copy(idx_cur, r, 0).start()
            return c
        lax.fori_loop(0, tc, one, 0)

    wait_slot(slot)
    xe = _rms(xbuf[slot], g_ref[...]).astype(BF16)
    acc = jnp.zeros((tc, D_MODEL), F32)
    nchunk = D_EXPERT // FFN_FCHUNK
    rows_per_chunk = tc // nchunk
    for c in range(nchunk):
        for r in range(c * rows_per_chunk, (c + 1) * rows_per_chunk):
            row_copy(idx_nxt, r, 1 - slot).start()
        fs = slice(c * FFN_FCHUNK, (c + 1) * FFN_FCHUNK)
        hg = _dot(xe, wg_ref[0, :, fs])
        hu = _dot(xe, wu_ref[0, :, fs])
        hid = (hg * jax.nn.sigmoid(hg)) * hu
        acc = acc + _dot(hid.astype(BF16), wd_ref[0, fs, :])
    y_ref[...] = (acc * gate_ref[0]).astype(BF16)

    @pl.when(step == total - 1)
    def _():
        wait_slot(1 - slot)


def _expert_ffn(x, g, idx, gate, wg, wu, wd, cap):
    tc = min(FFN_TC, cap)
    nt = cap // tc
    ntiles = N_EXPERTS * nt
    idx3 = idx.reshape(ntiles, 1, tc)
    gate3 = gate.reshape(N_EXPERTS, cap, 1)
    smem = lambda m: pl.BlockSpec((1, 1, tc), m, memory_space=pltpu.SMEM)
    wspec = lambda a, b: pl.BlockSpec((1, a, b), lambda e, i: (e, 0, 0))
    return pl.pallas_call(
        _ffn_body,
        grid=(N_EXPERTS, nt),
        in_specs=[smem(lambda e, i: (e * nt + i, 0, 0)),
                  smem(lambda e, i: (jnp.minimum(e * nt + i + 1, ntiles - 1), 0, 0)),
                  pl.BlockSpec(memory_space=pl.ANY),
                  pl.BlockSpec((1, D_MODEL), lambda e, i: (0, 0)),
                  pl.BlockSpec((1, tc, 1), lambda e, i: (e, i, 0)),
                  wspec(D_MODEL, D_EXPERT), wspec(D_MODEL, D_EXPERT), wspec(D_EXPERT, D_MODEL)],
        out_specs=pl.BlockSpec((tc, D_MODEL), lambda e, i: (e * nt + i, 0)),
        out_shape=jax.ShapeDtypeStruct((N_EXPERTS * cap, D_MODEL), BF16),
        scratch_shapes=[pltpu.VMEM((2, tc, D_MODEL), F32), pltpu.SemaphoreType.DMA((2,))],
        compiler_params=_cparams(("arbitrary", "arbitrary")),
        name="expert_ffn",
    )(idx3, idx3, x, g, gate3, wg, wu, wd)


COMB_TB = 512
COMB_CW = 128
COMB_ALIGN = 16


def _combine_body(boff_ref, x_ref, pos_ref, y_hbm, gfin_ref, o_ref, win, xwin, sem, xsem,
                  *, cap, final_norm):
    b = pl.program_id(0)
    total = N_EXPERTS * cap
    tb = x_ref.shape[0]
    cw = COMB_CW

    slot = b % 2

    def wstart(e, blk):
        s = jnp.bitwise_and(e * cap + boff_ref[e, blk], -COMB_ALIGN)
        return pl.multiple_of(jnp.minimum(s, total - cw), COMB_ALIGN)

    def window_copy(e, start, dst_slot):
        return pltpu.make_async_copy(y_hbm.at[pl.ds(start, cw)],
                                     win.at[dst_slot, pl.ds(e * cw, cw)], sem.at[dst_slot, e])

    @pl.when(b == 0)
    def _():
        for e in range(N_EXPERTS):
            window_copy(e, wstart(e, 0), 0).start()

    @pl.when(b + 1 < pl.num_programs(0))
    def _():
        for e in range(N_EXPERTS):
            window_copy(e, wstart(e, b + 1), 1 - slot).start()

    starts = [wstart(e, b) for e in range(N_EXPERTS)]
    lane = lax.broadcasted_iota(jnp.int32, (tb, cw), 1).astype(F32)
    pos = pos_ref[...]
    onehot = jnp.concatenate(
        [(pos[:, e:e + 1] - starts[e].astype(F32) == lane).astype(BF16) for e in range(N_EXPERTS)],
        axis=1)
    for e in range(N_EXPERTS):
        window_copy(e, starts[e], slot).wait()
    o_ref[...] = x_ref[...] + _dot(onehot, win[slot])

    for e in range(N_EXPERTS):
        end = e * cap + boff_ref[e, b + 1]
        extra = lax.shift_right_logical(jnp.maximum(end - starts[e] - 1, 0), int(math.log2(cw)))

        def more(c, carry, e=e):
            lo = starts[e] + c * cw
            s = pl.multiple_of(jnp.minimum(lo, total - cw), COMB_ALIGN)
            cp = pltpu.make_async_copy(y_hbm.at[pl.ds(s, cw)], xwin, xsem.at[0])
            cp.start()
            cp.wait()
            pe = pos[:, e:e + 1]
            oh = ((pe - s.astype(F32) == lane) & (pe >= lo.astype(F32))).astype(BF16)
            o_ref[...] += _dot(oh, xwin[...])
            return carry

        lax.fori_loop(1, extra + 1, more, 0)

    if final_norm:
        o_ref[...] = _rms(o_ref[...], gfin_ref[...])


def _combine(x, pos_t, boff, y, gfin, cap, final_norm):
    n = x.shape[0]
    tb = COMB_TB
    grid_spec = pltpu.PrefetchScalarGridSpec(
        num_scalar_prefetch=1,
        grid=(n // tb,),
        in_specs=[pl.BlockSpec((tb, D_MODEL), lambda b, s: (b, 0)),
                  pl.BlockSpec((tb, N_EXPERTS), lambda b, s: (b, 0)),
                  pl.BlockSpec(memory_space=pl.ANY),
                  pl.BlockSpec((1, D_MODEL), lambda b, s: (0, 0))],
        out_specs=pl.BlockSpec((tb, D_MODEL), lambda b, s: (b, 0)),
        scratch_shapes=[pltpu.VMEM((2, N_EXPERTS * COMB_CW, D_MODEL), BF16),
                        pltpu.VMEM((COMB_CW, D_MODEL), BF16),
                        pltpu.SemaphoreType.DMA((2, N_EXPERTS)),
                        pltpu.SemaphoreType.DMA((1,))],
    )
    return pl.pallas_call(
        functools.partial(_combine_body, cap=cap, final_norm=final_norm),
        grid_spec=grid_spec,
        out_shape=jax.ShapeDtypeStruct((n, D_MODEL), F32),
        compiler_params=_cparams(("arbitrary",)),
        name="ec_combine",
    )(boff, x, pos_t, y, gfin)


def _moe(xn, aff_t, ffn_norm, wg, wu, wd, gfin, final_norm):
    n = xn.shape[0]
    cap = EC_CAPACITY_FACTOR * n // N_EXPERTS
    idx, gate, pos, off = _route(aff_t, cap)
    y = _expert_ffn(xn, ffn_norm, idx, gate, wg, wu, wd, cap)
    pos_t = pos.reshape(N_EXPERTS, n).T
    rows_per_blk = COMB_TB // LANES
    boff = off[:, ::rows_per_blk, 0].astype(jnp.int32)
    boff = jnp.concatenate([boff, jnp.full((N_EXPERTS, 1), cap, jnp.int32)], axis=1)
    return _combine(xn, pos_t, boff, y, gfin, cap, final_norm)


def _expand_kv_columns(w_qkv):
    qd = SW_HEADS * HEAD_DIM
    kd = SW_KV_HEADS * HEAD_DIM
    head_of = jnp.arange(SW_HEADS) // (SW_HEADS // SW_KV_HEADS)
    cols = (head_of[:, None] * HEAD_DIM + jnp.arange(HEAD_DIM)[None, :]).reshape(-1)
    return jnp.concatenate([w_qkv[:, :qd], w_qkv[:, qd:qd + kd][:, cols],
                            w_qkv[:, qd + kd:][:, cols]], axis=1)


def _run_group(x3, layers, final_norm_g):
    bsz, seq_len, _ = x3.shape
    n = bsz * seq_len
    x = x3.reshape(n, D_MODEL)
    cos_t, sin_t = _rope_tables(seq_len)
    gfin = final_norm_g.reshape(1, D_MODEL).astype(F32)
    for i, lp in enumerate(layers):
        kind = MIXER_PATTERN[i % len(MIXER_PATTERN)]
        q, k, v = _qkv_proj(x, lp["mix_norm"], lp["w_qkv"], cos_t, sin_t, seq_len, kind != "nat")
        if kind == "nat":
            o = _nat_attention(q, k, v, lp["nat_bias"], bsz, seq_len)
        elif kind == "swa":
            o = _swa_attention(q, k, v, lp["sink"], bsz, seq_len)
        else:
            lambda_init = 0.8 - 0.6 * math.exp(-0.3 * i)
            o = _diff_attention(q, k, v, lp["lam4"], lp["subln"], bsz, seq_len, lambda_init)
        xn, aff_t = _oproj_router(o, x, lp["w_o"], lp["ffn_norm"], lp["w_router_t"])
        x = _moe(xn, aff_t, lp["ffn_norm"], lp["w_gate"], lp["w_up"], lp["w_down"],
                 gfin, i == len(layers) - 1)
    return x.reshape(bsz, seq_len, D_MODEL)


def _prepare_layer(i, lp):
    kind = MIXER_PATTERN[i % len(MIXER_PATTERN)]
    out = dict(
        mix_norm=lp["mix_norm"].reshape(1, D_MODEL).astype(F32),
        ffn_norm=lp["ffn_norm"].reshape(1, D_MODEL).astype(F32),
        w_o=lp["w_o"].astype(BF16),
        w_router_t=lp["w_router"].T.astype(BF16),
        w_gate=lp["w_gate"].astype(BF16), w_up=lp["w_up"].astype(BF16),
        w_down=lp["w_down"].astype(BF16),
    )
    if kind == "swa":
        out["w_qkv"] = _expand_kv_columns(lp["w_qkv"]).astype(BF16)
        out["sink"] = lp["sink"]
    else:
        out["w_qkv"] = lp["w_qkv"].astype(BF16)
    if kind == "nat":
        out["nat_bias"] = _nat_bias_table(lp["rpb"])
    if kind == "diff":
        out["lam4"] = jnp.stack([lp["lambda_q1"], lp["lambda_k1"], lp["lambda_q2"],
                                 lp["lambda_k2"]]).astype(F32)
        out["subln"] = lp["subln"].reshape(1, LANES).astype(F32)
    return out


def kernel(x_prompt, x_sample, l0_mix_norm, l0_w_qkv, l0_rpb, l0_w_o, l0_ffn_norm, l0_w_router, l0_w_gate, l0_w_up, l0_w_down, l1_mix_norm, l1_w_qkv, l1_sink, l1_w_o, l1_ffn_norm, l1_w_router, l1_w_gate, l1_w_up, l1_w_down, l2_mix_norm, l2_w_qkv, l2_lambda_q1, l2_lambda_k1, l2_lambda_q2, l2_lambda_k2, l2_subln, l2_w_o, l2_ffn_norm, l2_w_router, l2_w_gate, l2_w_up, l2_w_down, l3_mix_norm, l3_w_qkv, l3_rpb, l3_w_o, l3_ffn_norm, l3_w_router, l3_w_gate, l3_w_up, l3_w_down, final_norm):
    layers = [
        dict(mix_norm=l0_mix_norm, w_qkv=l0_w_qkv, rpb=l0_rpb, w_o=l0_w_o, ffn_norm=l0_ffn_norm,
             w_router=l0_w_router, w_gate=l0_w_gate, w_up=l0_w_up, w_down=l0_w_down),
        dict(mix_norm=l1_mix_norm, w_qkv=l1_w_qkv, sink=l1_sink, w_o=l1_w_o, ffn_norm=l1_ffn_norm,
             w_router=l1_w_router, w_gate=l1_w_gate, w_up=l1_w_up, w_down=l1_w_down),
        dict(mix_norm=l2_mix_norm, w_qkv=l2_w_qkv, lambda_q1=l2_lambda_q1, lambda_k1=l2_lambda_k1,
             lambda_q2=l2_lambda_q2, lambda_k2=l2_lambda_k2, subln=l2_subln, w_o=l2_w_o,
             ffn_norm=l2_ffn_norm, w_router=l2_w_router, w_gate=l2_w_gate, w_up=l2_w_up,
             w_down=l2_w_down),
        dict(mix_norm=l3_mix_norm, w_qkv=l3_w_qkv, rpb=l3_rpb, w_o=l3_w_o, ffn_norm=l3_ffn_norm,
             w_router=l3_w_router, w_gate=l3_w_gate, w_up=l3_w_up, w_down=l3_w_down),
    ]
    layers = [_prepare_layer(i, lp) for i, lp in enumerate(layers)]
    y_prompt = _run_group(x_prompt, layers, final_norm)
    y_sample = _run_group(x_sample, layers, final_norm)
    return (y_prompt, y_sample)
```

```python
import functools
import math

import jax
import jax.numpy as jnp
import numpy as np
from jax import lax
from jax.experimental import pallas as pl
from jax.experimental.pallas import tpu as pltpu

D_MODEL = 1024
DEPTH = 4
GRID_W = 64
NA_HEADS = 16
NA_WIN_H = 8
NA_WIN_W = 16
SW_HEADS = 16
SW_KV_HEADS = 4
SW_BLOCK = 128
DIFF_HEADS = 8
HEAD_DIM = 64
N_EXPERTS = 16
EC_CAPACITY_FACTOR = 2
D_EXPERT = 2048
ROPE_THETA = 10000.0
NORM_EPS = 1e-6
MIXER_PATTERN = ("nat", "swa", "diff")

LANES = 128
VMEM_LIMIT = 56 * 1024 * 1024

BF16 = jnp.bfloat16
F32 = jnp.float32
NEG_INF = float("-inf")


def _cparams(sem):
    return pltpu.CompilerParams(dimension_semantics=sem, vmem_limit_bytes=VMEM_LIMIT)


def _dot(a, b):
    return jnp.dot(a, b, preferred_element_type=F32)


def _dot_nt(a, b):
    return lax.dot_general(a, b, (((1,), (1,)), ((), ())), preferred_element_type=F32)


def _rms(x, g):
    return x * lax.rsqrt(jnp.mean(x * x, axis=-1, keepdims=True) + NORM_EPS) * g


PROJ_TM = 512


def _qkv_body(x_ref, g_ref, w_ref, cos_ref, sin_ref, q_ref, k_ref, v_ref, *, use_rope):
    h = _rms(x_ref[...], g_ref[...])
    y = _dot(h.astype(BF16), w_ref[...])
    tm = y.shape[0]
    lane = lax.broadcasted_iota(jnp.int32, (tm, LANES), 1)
    first_half = (lane % HEAD_DIM) < (HEAD_DIM // 2)
    if use_rope:
        cos = cos_ref[...]
        sin = sin_ref[...]
    outs = (q_ref, k_ref, v_ref)
    for part in range(3):
        for c in range(D_MODEL // LANES):
            col = part * D_MODEL + c * LANES
            blk = y[:, col:col + LANES]
            if use_rope and part < 2:
                rot = jnp.where(first_half,
                                pltpu.roll(blk, LANES - HEAD_DIM // 2, 1),
                                pltpu.roll(blk, HEAD_DIM // 2, 1))
                blk = blk * cos + rot * sin
            if part == 0:
                blk = blk * (HEAD_DIM ** -0.5)
            outs[part][:, c * LANES:(c + 1) * LANES] = blk.astype(BF16)


def _qkv_proj(x, g, w, cos, sin, seq_len, use_rope):
    n = x.shape[0]
    tm = PROJ_TM
    tblocks = seq_len // tm
    out = jax.ShapeDtypeStruct((n, D_MODEL), BF16)
    row = lambda i: (i, 0)
    return pl.pallas_call(
        functools.partial(_qkv_body, use_rope=use_rope),
        grid=(n // tm,),
        in_specs=[
            pl.BlockSpec((tm, D_MODEL), row),
            pl.BlockSpec((1, D_MODEL), lambda i: (0, 0)),
            pl.BlockSpec((D_MODEL, 3 * D_MODEL), lambda i: (0, 0)),
            pl.BlockSpec((tm, LANES), lambda i: (i % tblocks, 0)),
            pl.BlockSpec((tm, LANES), lambda i: (i % tblocks, 0)),
        ],
        out_specs=[pl.BlockSpec((tm, D_MODEL), row)] * 3,
        out_shape=[out, out, out],
        compiler_params=_cparams(("parallel",)),
        name="qkv_proj",
    )(x, g, w, cos, sin)


def _rope_tables(seq_len):
    inv = ROPE_THETA ** (-jnp.arange(0, HEAD_DIM, 2, dtype=F32) / HEAD_DIM)
    ang = jnp.arange(seq_len, dtype=F32)[:, None] * inv[None, :]
    cos = jnp.cos(ang)
    sin = jnp.sin(ang)
    cos_t = jnp.tile(jnp.concatenate([cos, cos], axis=1), (1, LANES // HEAD_DIM))
    sin_t = jnp.tile(jnp.concatenate([-sin, sin], axis=1), (1, LANES // HEAD_DIM))
    return cos_t, sin_t


def _pair_split(qp):
    lane = lax.broadcasted_iota(jnp.int32, qp.shape, 1)
    zero = jnp.zeros_like(qp)
    return jnp.concatenate([jnp.where(lane < HEAD_DIM, qp, zero),
                            jnp.where(lane >= HEAD_DIM, qp, zero)], axis=0)


def _pair_merge(o, m):
    lane = lax.broadcasted_iota(jnp.int32, (m, LANES), 1)
    return jnp.where(lane < HEAD_DIM, o[:m], o[m:])


NAT_ROWS_PER_STEP = 4
NAT_TQ = NAT_ROWS_PER_STEP * GRID_W
NAT_WIN_BLOCKS = 3
NAT_PAIRS_PER_STEP = 4


def _nat_body(q_ref, k0_ref, k1_ref, k2_ref, v0_ref, v1_ref, v2_ref, b_ref, o_ref):
    krefs = (k0_ref, k1_ref, k2_ref)
    vrefs = (v0_ref, v1_ref, v2_ref)
    for p in range(NAT_PAIRS_PER_STEP):
        cs = slice(p * LANES, (p + 1) * LANES)
        qs = _pair_split(q_ref[:, cs])
        s = [_dot_nt(qs, krefs[j][:, cs]) + b_ref[0, p, :, j * NAT_TQ:(j + 1) * NAT_TQ]
             for j in range(NAT_WIN_BLOCKS)]
        m = jnp.maximum(jnp.maximum(jnp.max(s[0], axis=-1, keepdims=True),
                                    jnp.max(s[1], axis=-1, keepdims=True)),
                        jnp.max(s[2], axis=-1, keepdims=True))
        e = [jnp.exp(sj - m) for sj in s]
        l = (jnp.sum(e[0], axis=-1, keepdims=True) + jnp.sum(e[1], axis=-1, keepdims=True)
             + jnp.sum(e[2], axis=-1, keepdims=True))
        r = 1.0 / l
        o = (_dot((e[0] * r).astype(BF16), vrefs[0][:, cs])
             + _dot((e[1] * r).astype(BF16), vrefs[1][:, cs])
             + _dot((e[2] * r).astype(BF16), vrefs[2][:, cs]))
        o_ref[:, cs] = _pair_merge(o, NAT_TQ).astype(BF16)


def _nat_bias_table(rpb):
    cols = jnp.arange(GRID_W)
    col_start = jnp.clip(cols - NA_WIN_W // 2, 0, GRID_W - NA_WIN_W)
    col_in = (cols[None, :] >= col_start[:, None]) & (cols[None, :] < col_start[:, None] + NA_WIN_W)
    dc_idx = jnp.clip(cols[None, :] - cols[:, None] + NA_WIN_W - 1, 0, 2 * NA_WIN_W - 2)
    tz = jnp.where(col_in[None, None], rpb.astype(F32)[:, :, dc_idx], NEG_INF)
    n_dr = 2 * NA_WIN_H - 1
    tz = jnp.concatenate([tz, jnp.full((NA_HEADS, 1, GRID_W, GRID_W), NEG_INF, F32)], axis=1)
    win_rows = NAT_WIN_BLOCKS * NAT_ROWS_PER_STEP
    half = NA_WIN_H // 2
    d_idx = np.full((3, NAT_ROWS_PER_STEP, win_rows), n_dr, np.int32)
    for case in range(3):
        for a in range(NAT_ROWS_PER_STEP):
            j0, d0 = ((0, NA_WIN_H - 1 - a), (a, NA_WIN_H - 1 - half), (half, NA_WIN_H - 1 - half - a))[case]
            for jj in range(NA_WIN_H):
                d_idx[case, a, j0 + jj] = d0 + jj
    t = tz[:, d_idx]
    t = jnp.transpose(t, (1, 0, 2, 4, 3, 5))
    return t.reshape(3, NA_HEADS // 2, 2 * NAT_TQ, win_rows * GRID_W)


def _nat_attention(q, k, v, bias, bsz, seq_len):
    n = q.shape[0]
    rows = seq_len // GRID_W
    nb = rows // NAT_ROWS_PER_STEP
    assert rows % NAT_ROWS_PER_STEP == 0 and nb >= NAT_WIN_BLOCKS
    pp = NAT_PAIRS_PER_STEP
    width = pp * LANES

    def win(j):
        return lambda hp, b, i: (b * nb + jnp.clip(i - 1, 0, nb - NAT_WIN_BLOCKS) + j, hp)

    def bias_map(hp, b, i):
        case = jnp.where(i == 0, 0, jnp.where(i == nb - 1, 2, 1))
        return (case, hp, 0, 0)

    qmap = lambda hp, b, i: (b * nb + i, hp)
    blk = lambda m: pl.BlockSpec((NAT_TQ, width), m)
    return pl.pallas_call(
        _nat_body,
        grid=(NA_HEADS // 2 // pp, bsz, nb),
        in_specs=[blk(qmap), blk(win(0)), blk(win(1)), blk(win(2)),
                  blk(win(0)), blk(win(1)), blk(win(2)),
                  pl.BlockSpec((1, pp, 2 * NAT_TQ, NAT_WIN_BLOCKS * NAT_TQ), bias_map)],
        out_specs=blk(qmap),
        out_shape=jax.ShapeDtypeStruct((n, D_MODEL), BF16),
        compiler_params=_cparams(("parallel", "parallel", "parallel")),
        name="nat_attention",
    )(q, k, k, k, v, v, v, bias)


def _swa_body(sink_ref, q_ref, k0_ref, k1_ref, k2_ref, v0_ref, v1_ref, v2_ref, o_ref, *, nblk):
    i = pl.program_id(1)
    tb = SW_BLOCK
    qq = lax.broadcasted_iota(jnp.int32, (2 * tb, tb), 0) % tb
    kk = lax.broadcasted_iota(jnp.int32, (2 * tb, tb), 1)
    valid_prev = (kk >= qq) & (i > 0)
    valid_next = (kk <= qq) & (i < nblk - 1)
    row = lax.broadcasted_iota(jnp.int32, (2 * tb, 1), 0)
    for w in range(SW_HEADS // 2):
        cs = slice(w * LANES, (w + 1) * LANES)
        qs = _pair_split(q_ref[:, cs])
        s0 = jnp.where(valid_prev, _dot_nt(qs, k0_ref[:, cs]), NEG_INF)
        s1 = _dot_nt(qs, k1_ref[:, cs])
        s2 = jnp.where(valid_next, _dot_nt(qs, k2_ref[:, cs]), NEG_INF)
        sink = jnp.where(row < tb, sink_ref[2 * w], sink_ref[2 * w + 1])
        m = jnp.maximum(jnp.maximum(jnp.max(s0, axis=-1, keepdims=True),
                                    jnp.max(s1, axis=-1, keepdims=True)),
                        jnp.maximum(jnp.max(s2, axis=-1, keepdims=True), sink))
        e0 = jnp.exp(s0 - m)
        e1 = jnp.exp(s1 - m)
        e2 = jnp.exp(s2 - m)
        l = (jnp.sum(e0, axis=-1, keepdims=True) + jnp.sum(e1, axis=-1, keepdims=True)
             + jnp.sum(e2, axis=-1, keepdims=True) + jnp.exp(sink - m))
        o = (_dot((e0 / l).astype(BF16), v0_ref[:, cs]) + _dot((e1 / l).astype(BF16), v1_ref[:, cs])
             + _dot((e2 / l).astype(BF16), v2_ref[:, cs]))
        o_ref[:, cs] = _pair_merge(o, tb).astype(BF16)


def _swa_attention(q, k, v, sink, bsz, seq_len):
    n = q.shape[0]
    nblk = seq_len // SW_BLOCK
    cur = lambda b, i: (b * nblk + i, 0)
    prev = lambda b, i: (b * nblk + jnp.maximum(i - 1, 0), 0)
    nxt = lambda b, i: (b * nblk + jnp.minimum(i + 1, nblk - 1), 0)
    blk = lambda m: pl.BlockSpec((SW_BLOCK, D_MODEL), m)
    return pl.pallas_call(
        functools.partial(_swa_body, nblk=nblk),
        grid=(bsz, nblk),
        in_specs=[pl.BlockSpec(memory_space=pltpu.SMEM),
                  blk(cur), blk(prev), blk(cur), blk(nxt), blk(prev), blk(cur), blk(nxt)],
        out_specs=blk(cur),
        out_shape=jax.ShapeDtypeStruct((n, D_MODEL), BF16),
        compiler_params=_cparams(("parallel", "parallel")),
        name="swa_attention",
    )(sink.astype(F32), q, k, k, k, v, v, v)


DIFF_TQ = 1024
DIFF_TK = 2048


def _diff_body(q_ref, k_ref, v_ref, lam_ref, subln_ref, o_ref, qs_sc, m_sc, l_sc, acc_sc,
               *, lambda_init):
    kv = pl.program_id(3)
    tq = q_ref.shape[0]
    tk = k_ref.shape[0]

    @pl.when(kv == 0)
    def _():
        qs_sc[...] = _pair_split(q_ref[...])
        m_sc[...] = jnp.full(m_sc.shape, NEG_INF, F32)
        l_sc[...] = jnp.zeros(l_sc.shape, F32)
        acc_sc[...] = jnp.zeros(acc_sc.shape, F32)

    s = _dot_nt(qs_sc[...], k_ref[...])
    m_prev = m_sc[...]
    m_new = jnp.maximum(m_prev, jnp.max(s, axis=-1, keepdims=True))
    alpha = jnp.exp(m_prev - m_new)
    p = jnp.exp(s - jnp.concatenate([m_new] * (tk // LANES), axis=1))
    l_sc[...] = alpha * l_sc[...] + jnp.sum(p, axis=-1, keepdims=True)
    acc_sc[...] = alpha * acc_sc[...] + _dot(p.astype(BF16), v_ref[...])
    m_sc[...] = m_new

    @pl.when(kv == pl.num_programs(3) - 1)
    def _():
        o12 = acc_sc[...] / l_sc[...]
        lam4 = lam_ref[...]
        lam = (jnp.exp(jnp.sum(lam4[0:1] * lam4[1:2], axis=-1, keepdims=True))
               - jnp.exp(jnp.sum(lam4[2:3] * lam4[3:4], axis=-1, keepdims=True)) + lambda_init)
        o = o12[:tq] - lam * o12[tq:]
        o = _rms(o, subln_ref[...]) * (1.0 - lambda_init)
        o_ref[...] = o.astype(BF16)


def _diff_attention(q, k, v, lam4, subln, bsz, seq_len, lambda_init):
    n = q.shape[0]
    tq = min(DIFF_TQ, seq_len)
    tk = min(DIFF_TK, seq_len)
    nq = seq_len // tq
    nk = seq_len // tk
    return pl.pallas_call(
        functools.partial(_diff_body, lambda_init=lambda_init),
        grid=(bsz, DIFF_HEADS, nq, nk),
        in_specs=[
            pl.BlockSpec((tq, LANES), lambda b, h, qi, ki: (b * nq + qi, h)),
            pl.BlockSpec((tk, LANES), lambda b, h, qi, ki: (b * nk + ki, h)),
            pl.BlockSpec((tk, LANES), lambda b, h, qi, ki: (b * nk + ki, h)),
            pl.BlockSpec((4, HEAD_DIM), lambda b, h, qi, ki: (0, 0)),
            pl.BlockSpec((1, LANES), lambda b, h, qi, ki: (0, 0)),
        ],
        out_specs=pl.BlockSpec((tq, LANES), lambda b, h, qi, ki: (b * nq + qi, h)),
        out_shape=jax.ShapeDtypeStruct((n, D_MODEL), BF16),
        scratch_shapes=[pltpu.VMEM((2 * tq, LANES), BF16), pltpu.VMEM((2 * tq, LANES), F32),
                        pltpu.VMEM((2 * tq, LANES), F32), pltpu.VMEM((2 * tq, LANES), F32)],
        compiler_params=_cparams(("parallel", "parallel", "parallel", "arbitrary")),
        name="diff_attention",
    )(q, k, v, lam4, subln)


def _oproj_body(o_ref, x_ref, wo_ref, g_ref, wr_ref, xn_ref, aff_ref):
    xn = x_ref[...] + _dot(o_ref[...], wo_ref[...])
    xn_ref[...] = xn
    h = _rms(xn, g_ref[...]).astype(BF16)
    logits = _dot_nt(wr_ref[...], h)
    m = jnp.max(logits, axis=0, keepdims=True)
    e = jnp.exp(logits - m)
    aff_ref[...] = e / jnp.sum(e, axis=0, keepdims=True)


OPROJ_TM = 512


def _oproj_router(o, x, wo, g, wr_t):
    n = x.shape[0]
    tm = OPROJ_TM
    row = lambda i: (i, 0)
    const = lambda i: (0, 0)
    return pl.pallas_call(
        _oproj_body,
        grid=(n // tm,),
        in_specs=[pl.BlockSpec((tm, D_MODEL), row), pl.BlockSpec((tm, D_MODEL), row),
                  pl.BlockSpec((D_MODEL, D_MODEL), const), pl.BlockSpec((1, D_MODEL), const),
                  pl.BlockSpec((N_EXPERTS, D_MODEL), const)],
        out_specs=[pl.BlockSpec((tm, D_MODEL), row), pl.BlockSpec((N_EXPERTS, tm), lambda i: (0, i))],
        out_shape=[jax.ShapeDtypeStruct((n, D_MODEL), F32),
                   jax.ShapeDtypeStruct((N_EXPERTS, n), F32)],
        compiler_params=_cparams(("parallel",)),
        name="oproj_router",
    )(o, x, wo, g, wr_t)


def _route_body(a_ref, idx_ref, gate_ref, pos_ref, off_ref, *, cap):
    e_id = pl.program_id(0)
    a = a_ref[0]
    nr = a.shape[0]
    bits = pltpu.bitcast(a, jnp.int32)

    def count(mask):
        c = jnp.sum(mask.astype(F32), axis=1, keepdims=True)
        return jnp.sum(c, axis=0, keepdims=True)

    def search(step, thr):
        cand = thr | lax.shift_left(jnp.int32(1), 30 - step)
        return jnp.where(count(bits >= cand) >= cap, cand, thr)

    thr = lax.fori_loop(0, 31, search, jnp.zeros((1, 1), jnp.int32))
    gt = bits > thr
    eq = bits == thr
    need = cap - count(gt)

    li = lax.broadcasted_iota(jnp.int32, (LANES, LANES), 0)
    lj = lax.broadcasted_iota(jnp.int32, (LANES, LANES), 1)
    upper_incl = (li <= lj).astype(BF16)
    ri = lax.broadcasted_iota(jnp.int32, (nr, nr), 0)
    rj = lax.broadcasted_iota(jnp.int32, (nr, nr), 1)
    lower_strict = (rj < ri).astype(BF16)

    def prefix(mask):
        w = _dot(mask.astype(BF16), upper_incl)
        tot = jnp.broadcast_to(w[:, LANES - 1:LANES], (nr, LANES))
        off = _dot(lower_strict, tot.astype(BF16))
        return w, tot, off

    w_eq, _, off_eq = prefix(eq)
    sel = gt | (eq & (off_eq + w_eq <= need))
    w, tot, off = prefix(sel)
    base = (e_id * cap).astype(F32)
    pos_ref[0] = jnp.where(sel, base + off + w - 1.0, -1.0)
    off_ref[0] = off

    off_excl = off[:, 0:1]
    off_incl = off_excl + tot[:, 0:1]
    w_t = w.T.astype(BF16)
    a_t = a.T
    a_hi_t = a_t.astype(BF16)
    r1 = a_t - a_hi_t.astype(F32)
    a_mid_t = r1.astype(BF16)
    a_lo_t = (r1 - a_mid_t.astype(F32)).astype(BF16)
    st = idx_ref.shape[2]
    r_col = lax.broadcasted_iota(jnp.int32, (nr, st), 0).astype(F32)
    l_col = lax.broadcasted_iota(jnp.int32, (LANES, st), 0).astype(F32)

    def slot_tile(t, carry):
        j = (t * st + lax.broadcasted_iota(jnp.int32, (1, st), 1)).astype(F32)
        hit = (off_excl <= j) & (j < off_incl)
        hit_b = hit.astype(BF16)
        k = j - jnp.sum(jnp.where(hit, off_excl, 0.0), axis=0, keepdims=True)
        row = jnp.sum(jnp.where(hit, r_col, 0.0), axis=0, keepdims=True)
        wsel = _dot(w_t, hit_b)
        lane = jnp.sum((wsel <= k).astype(F32), axis=0, keepdims=True)
        asel = _dot(a_hi_t, hit_b) + _dot(a_mid_t, hit_b) + _dot(a_lo_t, hit_b)
        gate = jnp.sum(jnp.where(l_col == lane, asel, 0.0), axis=0, keepdims=True)
        idx_ref[0, pl.ds(t, 1), :] = (row * LANES + lane).astype(jnp.int32)
        gate_ref[0, pl.ds(t, 1), :] = gate
        return carry

    lax.fori_loop(0, cap // st, slot_tile, 0)


ROUTE_SLOT_TILE = 512


def _route(aff_t, cap):
    n = aff_t.shape[1]
    nr = n // LANES
    st = min(ROUTE_SLOT_TILE, cap)
    a3 = aff_t.reshape(N_EXPERTS, nr, LANES)
    blk = lambda r, c=LANES: pl.BlockSpec((1, r, c), lambda e: (e, 0, 0))
    return pl.pallas_call(
        functools.partial(_route_body, cap=cap),
        grid=(N_EXPERTS,),
        in_specs=[blk(nr)],
        out_specs=[blk(cap // st, st), blk(cap // st, st), blk(nr), blk(nr)],
        out_shape=[jax.ShapeDtypeStruct((N_EXPERTS, cap // st, st), jnp.int32),
                   jax.ShapeDtypeStruct((N_EXPERTS, cap // st, st), F32),
                   jax.ShapeDtypeStruct((N_EXPERTS, nr, LANES), F32),
                   jax.ShapeDtypeStruct((N_EXPERTS, nr, LANES), F32)],
        compiler_params=_cparams(("parallel",)),
        name="ec_route",
    )(a3)


FFN_TC = 512
FFN_FCHUNK = 512


def _ffn_body(idx_cur, idx_nxt, x_hbm, g_ref, gate_ref, wg_ref, wu_ref, wd_ref, y_ref,
              xbuf, xe_sc, sem):
    tc = xbuf.shape[0]
    step = pl.program_id(0) * pl.num_programs(1) + pl.program_id(1)
    total = pl.num_programs(0) * pl.num_programs(1)

    def row_copy(idx, r):
        return pltpu.make_async_copy(x_hbm.at[pl.ds(idx[0, 0, r], 1)], xbuf.at[pl.ds(r, 1)],
                                     sem.at[0])

    def wait_rows():
        pltpu.make_async_copy(x_hbm.at[pl.ds(0, tc)], xbuf, sem.at[0]).wait()

    @pl.when(step == 0)
    def _():
        def one(r, c):
            row_copy(idx_cur, r).start()
            return c
        lax.fori_loop(0, tc, one, 0)

    wait_rows()
    xe_sc[...] = _rms(xbuf[...], g_ref[...]).astype(BF16)
    acc = jnp.zeros((tc, D_MODEL), F32)
    nchunk = D_EXPERT // FFN_FCHUNK
    rows_per_chunk = tc // nchunk
    for c in range(nchunk):
        for r in range(c * rows_per_chunk, (c + 1) * rows_per_chunk):
            row_copy(idx_nxt, r).start(priority=r % 2)
        fs = slice(c * FFN_FCHUNK, (c + 1) * FFN_FCHUNK)
        xe = xe_sc[...]
        hg = _dot(xe, wg_ref[0, :, fs])
        hu = _dot(xe, wu_ref[0, :, fs])
        hid = (hg * jax.nn.sigmoid(hg)) * hu
        acc = acc + _dot(hid.astype(BF16), wd_ref[0, fs, :])
    y_ref[...] = (acc * gate_ref[0]).astype(BF16)

    @pl.when(step == total - 1)
    def _():
        wait_rows()


def _expert_ffn(x, g, idx, gate, wg, wu, wd, cap):
    tc = min(FFN_TC, cap)
    nt = cap // tc
    ntiles = N_EXPERTS * nt
    idx3 = idx.reshape(ntiles, 1, tc)
    gate3 = gate.reshape(N_EXPERTS, cap, 1)
    smem = lambda m: pl.BlockSpec((1, 1, tc), m, memory_space=pltpu.SMEM)
    wspec = lambda a, b: pl.BlockSpec((1, a, b), lambda e, i: (e, 0, 0))
    return pl.pallas_call(
        _ffn_body,
        grid=(N_EXPERTS, nt),
        in_specs=[smem(lambda e, i: (e * nt + i, 0, 0)),
                  smem(lambda e, i: (jnp.minimum(e * nt + i + 1, ntiles - 1), 0, 0)),
                  pl.BlockSpec(memory_space=pl.ANY),
                  pl.BlockSpec((1, D_MODEL), lambda e, i: (0, 0)),
                  pl.BlockSpec((1, tc, 1), lambda e, i: (e, i, 0)),
                  wspec(D_MODEL, D_EXPERT), wspec(D_MODEL, D_EXPERT), wspec(D_EXPERT, D_MODEL)],
        out_specs=pl.BlockSpec((tc, D_MODEL), lambda e, i: (e * nt + i, 0)),
        out_shape=jax.ShapeDtypeStruct((N_EXPERTS * cap, D_MODEL), BF16),
        scratch_shapes=[pltpu.VMEM((tc, D_MODEL), F32), pltpu.VMEM((tc, D_MODEL), BF16),
                        pltpu.SemaphoreType.DMA((1,))],
        compiler_params=_cparams(("arbitrary", "arbitrary")),
        name="expert_ffn",
    )(idx3, idx3, x, g, gate3, wg, wu, wd)


COMB_TB = 512
COMB_CW = 128
COMB_ALIGN = 16


def _combine_body(boff_ref, x_ref, pos_ref, y_hbm, gfin_ref, o_ref, win, xwin, sem, xsem,
                  *, cap, final_norm):
    b = pl.program_id(0)
    total = N_EXPERTS * cap
    tb = x_ref.shape[0]
    cw = COMB_CW

    slot = b % 2

    def wstart(e, blk):
        s = jnp.bitwise_and(e * cap + boff_ref[e, blk], -COMB_ALIGN)
        return pl.multiple_of(jnp.minimum(s, total - cw), COMB_ALIGN)

    def window_copy(e, start, dst_slot):
        return pltpu.make_async_copy(y_hbm.at[pl.ds(start, cw)],
                                     win.at[dst_slot, pl.ds(e * cw, cw)], sem.at[dst_slot, e])

    @pl.when(b == 0)
    def _():
        for e in range(N_EXPERTS):
            window_copy(e, wstart(e, 0), 0).start()

    @pl.when(b + 1 < pl.num_programs(0))
    def _():
        for e in range(N_EXPERTS):
            window_copy(e, wstart(e, b + 1), 1 - slot).start()

    starts = [wstart(e, b) for e in range(N_EXPERTS)]
    lane = lax.broadcasted_iota(jnp.int32, (tb, cw), 1).astype(F32)
    pos = pos_ref[...]
    onehot = jnp.concatenate(
        [(pos[:, e:e + 1] - starts[e].astype(F32) == lane).astype(BF16) for e in range(N_EXPERTS)],
        axis=1)
    for e in range(N_EXPERTS):
        window_copy(e, starts[e], slot).wait()
    o_ref[...] = x_ref[...] + _dot(onehot, win[slot])

    for e in range(N_EXPERTS):
        end = e * cap + boff_ref[e, b + 1]
        extra = lax.shift_right_logical(jnp.maximum(end - starts[e] - 1, 0), int(math.log2(cw)))

        def more(c, carry, e=e):
            lo = starts[e] + c * cw
            s = pl.multiple_of(jnp.minimum(lo, total - cw), COMB_ALIGN)
            cp = pltpu.make_async_copy(y_hbm.at[pl.ds(s, cw)], xwin, xsem.at[0])
            cp.start()
            cp.wait()
            pe = pos[:, e:e + 1]
            oh = ((pe - s.astype(F32) == lane) & (pe >= lo.astype(F32))).astype(BF16)
            o_ref[...] += _dot(oh, xwin[...])
            return carry

        lax.fori_loop(1, extra + 1, more, 0)

    if final_norm:
        o_ref[...] = _rms(o_ref[...], gfin_ref[...])


def _combine(x, pos_t, boff, y, gfin, cap, final_norm):
    n = x.shape[0]
    tb = COMB_TB
    grid_spec = pltpu.PrefetchScalarGridSpec(
        num_scalar_prefetch=1,
        grid=(n // tb,),
        in_specs=[pl.BlockSpec((tb, D_MODEL), lambda b, s: (b, 0)),
                  pl.BlockSpec((tb, N_EXPERTS), lambda b, s: (b, 0)),
                  pl.BlockSpec(memory_space=pl.ANY),
                  pl.BlockSpec((1, D_MODEL), lambda b, s: (0, 0))],
        out_specs=pl.BlockSpec((tb, D_MODEL), lambda b, s: (b, 0)),
        scratch_shapes=[pltpu.VMEM((2, N_EXPERTS * COMB_CW, D_MODEL), BF16),
                        pltpu.VMEM((COMB_CW, D_MODEL), BF16),
                        pltpu.SemaphoreType.DMA((2, N_EXPERTS)),
                        pltpu.SemaphoreType.DMA((1,))],
    )
    return pl.pallas_call(
        functools.partial(_combine_body, cap=cap, final_norm=final_norm),
        grid_spec=grid_spec,
        out_shape=jax.ShapeDtypeStruct((n, D_MODEL), F32),
        compiler_params=_cparams(("arbitrary",)),
        name="ec_combine",
    )(boff, x, pos_t, y, gfin)


def _moe(xn, aff_t, ffn_norm, wg, wu, wd, gfin, final_norm):
    n = xn.shape[0]
    cap = EC_CAPACITY_FACTOR * n // N_EXPERTS
    idx, gate, pos, off = _route(aff_t, cap)
    y = _expert_ffn(xn, ffn_norm, idx, gate, wg, wu, wd, cap)
    pos_t = pos.reshape(N_EXPERTS, n).T
    rows_per_blk = COMB_TB // LANES
    boff = off[:, ::rows_per_blk, 0].astype(jnp.int32)
    boff = jnp.concatenate([boff, jnp.full((N_EXPERTS, 1), cap, jnp.int32)], axis=1)
    return _combine(xn, pos_t, boff, y, gfin, cap, final_norm)


def _expand_kv_columns(w_qkv):
    qd = SW_HEADS * HEAD_DIM
    kd = SW_KV_HEADS * HEAD_DIM
    head_of = jnp.arange(SW_HEADS) // (SW_HEADS // SW_KV_HEADS)
    cols = (head_of[:, None] * HEAD_DIM + jnp.arange(HEAD_DIM)[None, :]).reshape(-1)
    return jnp.concatenate([w_qkv[:, :qd], w_qkv[:, qd:qd + kd][:, cols],
                            w_qkv[:, qd + kd:][:, cols]], axis=1)


def _run_group(x3, layers, final_norm_g):
    bsz, seq_len, _ = x3.shape
    n = bsz * seq_len
    x = x3.reshape(n, D_MODEL)
    cos_t, sin_t = _rope_tables(seq_len)
    gfin = final_norm_g.reshape(1, D_MODEL).astype(F32)
    for i, lp in enumerate(layers):
        kind = MIXER_PATTERN[i % len(MIXER_PATTERN)]
        q, k, v = _qkv_proj(x, lp["mix_norm"], lp["w_qkv"], cos_t, sin_t, seq_len, kind != "nat")
        if kind == "nat":
            o = _nat_attention(q, k, v, lp["nat_bias"], bsz, seq_len)
        elif kind == "swa":
            o = _swa_attention(q, k, v, lp["sink"], bsz, seq_len)
        else:
            lambda_init = 0.8 - 0.6 * math.exp(-0.3 * i)
            o = _diff_attention(q, k, v, lp["lam4"], lp["subln"], bsz, seq_len, lambda_init)
        xn, aff_t = _oproj_router(o, x, lp["w_o"], lp["ffn_norm"], lp["w_router_t"])
        x = _moe(xn, aff_t, lp["ffn_norm"], lp["w_gate"], lp["w_up"], lp["w_down"],
                 gfin, i == len(layers) - 1)
    return x.reshape(bsz, seq_len, D_MODEL)


def _prepare_layer(i, lp):
    kind = MIXER_PATTERN[i % len(MIXER_PATTERN)]
    out = dict(
        mix_norm=lp["mix_norm"].reshape(1, D_MODEL).astype(F32),
        ffn_norm=lp["ffn_norm"].reshape(1, D_MODEL).astype(F32),
        w_o=lp["w_o"].astype(BF16),
        w_router_t=lp["w_router"].T.astype(BF16),
        w_gate=lp["w_gate"].astype(BF16), w_up=lp["w_up"].astype(BF16),
        w_down=lp["w_down"].astype(BF16),
    )
    if kind == "swa":
        out["w_qkv"] = _expand_kv_columns(lp["w_qkv"]).astype(BF16)
        out["sink"] = lp["sink"]
    else:
        out["w_qkv"] = lp["w_qkv"].astype(BF16)
    if kind == "nat":
        out["nat_bias"] = _nat_bias_table(lp["rpb"])
    if kind == "diff":
        out["lam4"] = jnp.stack([lp["lambda_q1"], lp["lambda_k1"], lp["lambda_q2"],
                                 lp["lambda_k2"]]).astype(F32)
        out["subln"] = lp["subln"].reshape(1, LANES).astype(F32)
    return out


def kernel(x_prompt, x_sample, l0_mix_norm, l0_w_qkv, l0_rpb, l0_w_o, l0_ffn_norm, l0_w_router, l0_w_gate, l0_w_up, l0_w_down, l1_mix_norm, l1_w_qkv, l1_sink, l1_w_o, l1_ffn_norm, l1_w_router, l1_w_gate, l1_w_up, l1_w_down, l2_mix_norm, l2_w_qkv, l2_lambda_q1, l2_lambda_k1, l2_lambda_q2, l2_lambda_k2, l2_subln, l2_w_o, l2_ffn_norm, l2_w_router, l2_w_gate, l2_w_up, l2_w_down, l3_mix_norm, l3_w_qkv, l3_rpb, l3_w_o, l3_ffn_norm, l3_w_router, l3_w_gate, l3_w_up, l3_w_down, final_norm):
    layers = [
        dict(mix_norm=l0_mix_norm, w_qkv=l0_w_qkv, rpb=l0_rpb, w_o=l0_w_o, ffn_norm=l0_ffn_norm,
             w_router=l0_w_router, w_gate=l0_w_gate, w_up=l0_w_up, w_down=l0_w_down),
        dict(mix_norm=l1_mix_norm, w_qkv=l1_w_qkv, sink=l1_sink, w_o=l1_w_o, ffn_norm=l1_ffn_norm,
             w_router=l1_w_router, w_gate=l1_w_gate, w_up=l1_w_up, w_down=l1_w_down),
        dict(mix_norm=l2_mix_norm, w_qkv=l2_w_qkv, lambda_q1=l2_lambda_q1, lambda_k1=l2_lambda_k1,
             lambda_q2=l2_lambda_q2, lambda_k2=l2_lambda_k2, subln=l2_subln, w_o=l2_w_o,
             ffn_norm=l2_ffn_norm, w_router=l2_w_router, w_gate=l2_w_gate, w_up=l2_w_up,
             w_down=l2_w_down),
        dict(mix_norm=l3_mix_norm, w_qkv=l3_w_qkv, rpb=l3_rpb, w_o=l3_w_o, ffn_norm=l3_ffn_norm,
             w_router=l3_w_router, w_gate=l3_w_gate, w_up=l3_w_up, w_down=l3_w_down),
    ]
    layers = [_prepare_layer(i, lp) for i, lp in enumerate(layers)]
    y_prompt = _run_group(x_prompt, layers, final_norm)
    y_sample = _run_group(x_sample, layers, final_norm)
    return (y_prompt, y_sample)
```

```python
import functools
import math

import jax
import jax.numpy as jnp
import numpy as np
from jax import lax
from jax.experimental import pallas as pl
from jax.experimental.pallas import tpu as pltpu

D_MODEL = 1024
DEPTH = 4
GRID_W = 64
NA_HEADS = 16
NA_WIN_H = 8
NA_WIN_W = 16
SW_HEADS = 16
SW_KV_HEADS = 4
SW_BLOCK = 128
DIFF_HEADS = 8
HEAD_DIM = 64
N_EXPERTS = 16
EC_CAPACITY_FACTOR = 2
D_EXPERT = 2048
ROPE_THETA = 10000.0
NORM_EPS = 1e-6
MIXER_PATTERN = ("nat", "swa", "diff")

LANES = 128
VMEM_LIMIT = 56 * 1024 * 1024

BF16 = jnp.bfloat16
F32 = jnp.float32
NEG_INF = float("-inf")


def _cparams(sem):
    return pltpu.CompilerParams(dimension_semantics=sem, vmem_limit_bytes=VMEM_LIMIT)


def _dot(a, b):
    return jnp.dot(a, b, preferred_element_type=F32)


def _dot_nt(a, b):
    return lax.dot_general(a, b, (((1,), (1,)), ((), ())), preferred_element_type=F32)


def _rms(x, g):
    return x * lax.rsqrt(jnp.mean(x * x, axis=-1, keepdims=True) + NORM_EPS) * g


PROJ_TM = 512


def _qkv_body(x_ref, g_ref, w_ref, cos_ref, sin_ref, q_ref, k_ref, v_ref, *, use_rope):
    h = _rms(x_ref[...], g_ref[...])
    y = _dot(h.astype(BF16), w_ref[...])
    tm = y.shape[0]
    lane = lax.broadcasted_iota(jnp.int32, (tm, LANES), 1)
    first_half = (lane % HEAD_DIM) < (HEAD_DIM // 2)
    if use_rope:
        cos = cos_ref[...]
        sin = sin_ref[...]
    outs = (q_ref, k_ref, v_ref)
    for part in range(3):
        for c in range(D_MODEL // LANES):
            col = part * D_MODEL + c * LANES
            blk = y[:, col:col + LANES]
            if use_rope and part < 2:
                rot = jnp.where(first_half,
                                pltpu.roll(blk, LANES - HEAD_DIM // 2, 1),
                                pltpu.roll(blk, HEAD_DIM // 2, 1))
                blk = blk * cos + rot * sin
            if part == 0:
                blk = blk * (HEAD_DIM ** -0.5)
            outs[part][:, c * LANES:(c + 1) * LANES] = blk.astype(BF16)


def _qkv_proj(x, g, w, cos, sin, seq_len, use_rope):
    n = x.shape[0]
    tm = PROJ_TM
    tblocks = seq_len // tm
    out = jax.ShapeDtypeStruct((n, D_MODEL), BF16)
    row = lambda i: (i, 0)
    return pl.pallas_call(
        functools.partial(_qkv_body, use_rope=use_rope),
        grid=(n // tm,),
        in_specs=[
            pl.BlockSpec((tm, D_MODEL), row),
            pl.BlockSpec((1, D_MODEL), lambda i: (0, 0)),
            pl.BlockSpec((D_MODEL, 3 * D_MODEL), lambda i: (0, 0)),
            pl.BlockSpec((tm, LANES), lambda i: (i % tblocks, 0)),
            pl.BlockSpec((tm, LANES), lambda i: (i % tblocks, 0)),
        ],
        out_specs=[pl.BlockSpec((tm, D_MODEL), row)] * 3,
        out_shape=[out, out, out],
        compiler_params=_cparams(("parallel",)),
        name="qkv_proj",
    )(x, g, w, cos, sin)


def _rope_tables(seq_len):
    inv = ROPE_THETA ** (-jnp.arange(0, HEAD_DIM, 2, dtype=F32) / HEAD_DIM)
    ang = jnp.arange(seq_len, dtype=F32)[:, None] * inv[None, :]
    cos = jnp.cos(ang)
    sin = jnp.sin(ang)
    cos_t = jnp.tile(jnp.concatenate([cos, cos], axis=1), (1, LANES // HEAD_DIM))
    sin_t = jnp.tile(jnp.concatenate([-sin, sin], axis=1), (1, LANES // HEAD_DIM))
    return cos_t, sin_t


def _pair_split(qp):
    lane = lax.broadcasted_iota(jnp.int32, qp.shape, 1)
    zero = jnp.zeros_like(qp)
    return jnp.concatenate([jnp.where(lane < HEAD_DIM, qp, zero),
                            jnp.where(lane >= HEAD_DIM, qp, zero)], axis=0)


def _pair_merge(o, m):
    lane = lax.broadcasted_iota(jnp.int32, (m, LANES), 1)
    return jnp.where(lane < HEAD_DIM, o[:m], o[m:])


NAT_ROWS_PER_STEP = 4
NAT_TQ = NAT_ROWS_PER_STEP * GRID_W
NAT_WIN_BLOCKS = 3
NAT_PAIRS_PER_STEP = 4


def _nat_body(q_ref, k0_ref, k1_ref, k2_ref, v0_ref, v1_ref, v2_ref, b_ref, o_ref):
    krefs = (k0_ref, k1_ref, k2_ref)
    vrefs = (v0_ref, v1_ref, v2_ref)
    for p in range(NAT_PAIRS_PER_STEP):
        cs = slice(p * LANES, (p + 1) * LANES)
        qs = _pair_split(q_ref[:, cs])
        s = [_dot_nt(qs, krefs[j][:, cs]) + b_ref[0, p, :, j * NAT_TQ:(j + 1) * NAT_TQ]
             for j in range(NAT_WIN_BLOCKS)]
        m = jnp.maximum(jnp.maximum(jnp.max(s[0], axis=-1, keepdims=True),
                                    jnp.max(s[1], axis=-1, keepdims=True)),
                        jnp.max(s[2], axis=-1, keepdims=True))
        e = [jnp.exp(sj - m) for sj in s]
        l = (jnp.sum(e[0], axis=-1, keepdims=True) + jnp.sum(e[1], axis=-1, keepdims=True)
             + jnp.sum(e[2], axis=-1, keepdims=True))
        r = 1.0 / l
        o = (_dot((e[0] * r).astype(BF16), vrefs[0][:, cs])
             + _dot((e[1] * r).astype(BF16), vrefs[1][:, cs])
             + _dot((e[2] * r).astype(BF16), vrefs[2][:, cs]))
        o_ref[:, cs] = _pair_merge(o, NAT_TQ).astype(BF16)


def _nat_bias_table(rpb):
    cols = jnp.arange(GRID_W)
    col_start = jnp.clip(cols - NA_WIN_W // 2, 0, GRID_W - NA_WIN_W)
    col_in = (cols[None, :] >= col_start[:, None]) & (cols[None, :] < col_start[:, None] + NA_WIN_W)
    dc_idx = jnp.clip(cols[None, :] - cols[:, None] + NA_WIN_W - 1, 0, 2 * NA_WIN_W - 2)
    tz = jnp.where(col_in[None, None], rpb.astype(F32)[:, :, dc_idx], NEG_INF)
    n_dr = 2 * NA_WIN_H - 1
    tz = jnp.concatenate([tz, jnp.full((NA_HEADS, 1, GRID_W, GRID_W), NEG_INF, F32)], axis=1)
    win_rows = NAT_WIN_BLOCKS * NAT_ROWS_PER_STEP
    half = NA_WIN_H // 2
    d_idx = np.full((3, NAT_ROWS_PER_STEP, win_rows), n_dr, np.int32)
    for case in range(3):
        for a in range(NAT_ROWS_PER_STEP):
            j0, d0 = ((0, NA_WIN_H - 1 - a), (a, NA_WIN_H - 1 - half), (half, NA_WIN_H - 1 - half - a))[case]
            for jj in range(NA_WIN_H):
                d_idx[case, a, j0 + jj] = d0 + jj
    t = tz[:, d_idx]
    t = jnp.transpose(t, (1, 0, 2, 4, 3, 5))
    return t.reshape(3, NA_HEADS // 2, 2 * NAT_TQ, win_rows * GRID_W)


def _nat_attention(q, k, v, bias, bsz, seq_len):
    n = q.shape[0]
    rows = seq_len // GRID_W
    nb = rows // NAT_ROWS_PER_STEP
    assert rows % NAT_ROWS_PER_STEP == 0 and nb >= NAT_WIN_BLOCKS
    pp = NAT_PAIRS_PER_STEP
    width = pp * LANES

    def win(j):
        return lambda hp, b, i: (b * nb + jnp.clip(i - 1, 0, nb - NAT_WIN_BLOCKS) + j, hp)

    def bias_map(hp, b, i):
        case = jnp.where(i == 0, 0, jnp.where(i == nb - 1, 2, 1))
        return (case, hp, 0, 0)

    qmap = lambda hp, b, i: (b * nb + i, hp)
    blk = lambda m: pl.BlockSpec((NAT_TQ, width), m)
    return pl.pallas_call(
        _nat_body,
        grid=(NA_HEADS // 2 // pp, bsz, nb),
        in_specs=[blk(qmap), blk(win(0)), blk(win(1)), blk(win(2)),
                  blk(win(0)), blk(win(1)), blk(win(2)),
                  pl.BlockSpec((1, pp, 2 * NAT_TQ, NAT_WIN_BLOCKS * NAT_TQ), bias_map)],
        out_specs=blk(qmap),
        out_shape=jax.ShapeDtypeStruct((n, D_MODEL), BF16),
        compiler_params=_cparams(("parallel", "parallel", "parallel")),
        name="nat_attention",
    )(q, k, k, k, v, v, v, bias)


def _swa_body(sink_ref, q_ref, k0_ref, k1_ref, k2_ref, v0_ref, v1_ref, v2_ref, o_ref, *, nblk):
    i = pl.program_id(1)
    tb = SW_BLOCK
    group = SW_HEADS // SW_KV_HEADS
    qq = lax.broadcasted_iota(jnp.int32, (group * tb, 3 * tb), 0) % tb
    kk = lax.broadcasted_iota(jnp.int32, (group * tb, 3 * tb), 1) - tb
    valid = ((jnp.abs(kk - qq) <= tb) & ((kk >= 0) | (i > 0)) & ((kk < tb) | (i < nblk - 1)))
    row = lax.broadcasted_iota(jnp.int32, (group * tb, 1), 0)
    for u in range(SW_KV_HEADS):
        pairs = [slice((2 * u + t) * LANES, (2 * u + t + 1) * LANES) for t in range(group // 2)]
        qs = jnp.concatenate([_pair_split(q_ref[:, cs]) for cs in pairs], axis=0)
        kc = jnp.concatenate([k0_ref[:, pairs[0]], k1_ref[:, pairs[0]], k2_ref[:, pairs[0]]], axis=0)
        vc = jnp.concatenate([v0_ref[:, pairs[0]], v1_ref[:, pairs[0]], v2_ref[:, pairs[0]]], axis=0)
        s = jnp.where(valid, _dot_nt(qs, kc), NEG_INF)
        sink = sink_ref[group * u + group - 1]
        for h in reversed(range(group - 1)):
            sink = jnp.where(row < (h + 1) * tb, sink_ref[group * u + h], sink)
        m = jnp.maximum(jnp.max(s, axis=-1, keepdims=True), sink)
        e = jnp.exp(s - m)
        l = jnp.sum(e, axis=-1, keepdims=True) + jnp.exp(sink - m)
        o = _dot((e / l).astype(BF16), vc)
        for t, cs in enumerate(pairs):
            o_ref[:, cs] = _pair_merge(o[2 * t * tb:(2 * t + 2) * tb], tb).astype(BF16)


def _swa_attention(q, k, v, sink, bsz, seq_len):
    n = q.shape[0]
    nblk = seq_len // SW_BLOCK
    cur = lambda b, i: (b * nblk + i, 0)
    prev = lambda b, i: (b * nblk + jnp.maximum(i - 1, 0), 0)
    nxt = lambda b, i: (b * nblk + jnp.minimum(i + 1, nblk - 1), 0)
    blk = lambda m: pl.BlockSpec((SW_BLOCK, D_MODEL), m)
    return pl.pallas_call(
        functools.partial(_swa_body, nblk=nblk),
        grid=(bsz, nblk),
        in_specs=[pl.BlockSpec(memory_space=pltpu.SMEM),
                  blk(cur), blk(prev), blk(cur), blk(nxt), blk(prev), blk(cur), blk(nxt)],
        out_specs=blk(cur),
        out_shape=jax.ShapeDtypeStruct((n, D_MODEL), BF16),
        compiler_params=_cparams(("parallel", "parallel")),
        name="swa_attention",
    )(sink.astype(F32), q, k, k, k, v, v, v)


DIFF_TQ = 1024
DIFF_TK = 2048
DIFF_ROW_CHUNKS = 4


def _diff_body(q_ref, k_ref, v_ref, lam_ref, subln_ref, o_ref, qs_sc, m_sc, l_sc, acc_sc,
               *, lambda_init):
    kv = pl.program_id(3)
    tq = q_ref.shape[0]
    tk = k_ref.shape[0]

    @pl.when(kv == 0)
    def _():
        qs_sc[...] = _pair_split(q_ref[...])
        m_sc[...] = jnp.full(m_sc.shape, NEG_INF, F32)
        l_sc[...] = jnp.zeros(l_sc.shape, F32)
        acc_sc[...] = jnp.zeros(acc_sc.shape, F32)

    k = k_ref[...]
    v = v_ref[...]
    rows = 2 * tq // DIFF_ROW_CHUNKS
    for c in range(DIFF_ROW_CHUNKS):
        rs = slice(c * rows, (c + 1) * rows)
        s = _dot_nt(qs_sc[rs, :], k)
        m_prev = m_sc[rs, :]
        m_new = jnp.maximum(m_prev, jnp.max(s, axis=-1, keepdims=True))
        alpha = jnp.exp(m_prev - m_new)
        p = jnp.exp(s - jnp.concatenate([m_new] * (tk // LANES), axis=1))
        l_sc[rs, :] = alpha * l_sc[rs, :] + jnp.sum(p, axis=-1, keepdims=True)
        acc_sc[rs, :] = alpha * acc_sc[rs, :] + _dot(p.astype(BF16), v)
        m_sc[rs, :] = m_new

    @pl.when(kv == pl.num_programs(3) - 1)
    def _():
        o12 = acc_sc[...] / l_sc[...]
        lam4 = lam_ref[...]
        lam = (jnp.exp(jnp.sum(lam4[0:1] * lam4[1:2], axis=-1, keepdims=True))
               - jnp.exp(jnp.sum(lam4[2:3] * lam4[3:4], axis=-1, keepdims=True)) + lambda_init)
        o = o12[:tq] - lam * o12[tq:]
        o = _rms(o, subln_ref[...]) * (1.0 - lambda_init)
        o_ref[...] = o.astype(BF16)


def _diff_attention(q, k, v, lam4, subln, bsz, seq_len, lambda_init):
    n = q.shape[0]
    tq = min(DIFF_TQ, seq_len)
    tk = min(DIFF_TK, seq_len)
    nq = seq_len // tq
    nk = seq_len // tk
    return pl.pallas_call(
        functools.partial(_diff_body, lambda_init=lambda_init),
        grid=(bsz, DIFF_HEADS, nq, nk),
        in_specs=[
            pl.BlockSpec((tq, LANES), lambda b, h, qi, ki: (b * nq + qi, h)),
            pl.BlockSpec((tk, LANES), lambda b, h, qi, ki: (b * nk + ki, h)),
            pl.BlockSpec((tk, LANES), lambda b, h, qi, ki: (b * nk + ki, h)),
            pl.BlockSpec((4, HEAD_DIM), lambda b, h, qi, ki: (0, 0)),
            pl.BlockSpec((1, LANES), lambda b, h, qi, ki: (0, 0)),
        ],
        out_specs=pl.BlockSpec((tq, LANES), lambda b, h, qi, ki: (b * nq + qi, h)),
        out_shape=jax.ShapeDtypeStruct((n, D_MODEL), BF16),
        scratch_shapes=[pltpu.VMEM((2 * tq, LANES), BF16), pltpu.VMEM((2 * tq, LANES), F32),
                        pltpu.VMEM((2 * tq, LANES), F32), pltpu.VMEM((2 * tq, LANES), F32)],
        compiler_params=_cparams(("parallel", "parallel", "parallel", "arbitrary")),
        name="diff_attention",
    )(q, k, v, lam4, subln)


def _oproj_body(o_ref, x_ref, wo_ref, g_ref, wr_ref, xn_ref, aff_ref):
    xn = x_ref[...] + _dot(o_ref[...], wo_ref[...])
    xn_ref[...] = xn
    h = _rms(xn, g_ref[...]).astype(BF16)
    logits = _dot_nt(wr_ref[...], h)
    m = jnp.max(logits, axis=0, keepdims=True)
    e = jnp.exp(logits - m)
    aff_ref[...] = e / jnp.sum(e, axis=0, keepdims=True)


OPROJ_TM = 512


def _oproj_router(o, x, wo, g, wr_t):
    n = x.shape[0]
    tm = OPROJ_TM
    row = lambda i: (i, 0)
    const = lambda i: (0, 0)
    return pl.pallas_call(
        _oproj_body,
        grid=(n // tm,),
        in_specs=[pl.BlockSpec((tm, D_MODEL), row), pl.BlockSpec((tm, D_MODEL), row),
                  pl.BlockSpec((D_MODEL, D_MODEL), const), pl.BlockSpec((1, D_MODEL), const),
                  pl.BlockSpec((N_EXPERTS, D_MODEL), const)],
        out_specs=[pl.BlockSpec((tm, D_MODEL), row), pl.BlockSpec((N_EXPERTS, tm), lambda i: (0, i))],
        out_shape=[jax.ShapeDtypeStruct((n, D_MODEL), F32),
                   jax.ShapeDtypeStruct((N_EXPERTS, n), F32)],
        compiler_params=_cparams(("parallel",)),
        name="oproj_router",
    )(o, x, wo, g, wr_t)


def _route_body(a_ref, idx_ref, gate_ref, pos_ref, off_ref, *, cap):
    e_id = pl.program_id(0)
    a = a_ref[0]
    nr = a.shape[0]
    bits = pltpu.bitcast(a, jnp.int32)

    def count(mask):
        c = jnp.sum(mask.astype(F32), axis=1, keepdims=True)
        return jnp.sum(c, axis=0, keepdims=True)

    def search(step, thr):
        cand = thr | lax.shift_left(jnp.int32(1), 30 - step)
        return jnp.where(count(bits >= cand) >= cap, cand, thr)

    thr = lax.fori_loop(0, 31, search, jnp.zeros((1, 1), jnp.int32))
    gt = bits > thr
    eq = bits == thr
    need = cap - count(gt)

    li = lax.broadcasted_iota(jnp.int32, (LANES, LANES), 0)
    lj = lax.broadcasted_iota(jnp.int32, (LANES, LANES), 1)
    upper_incl = (li <= lj).astype(BF16)
    ri = lax.broadcasted_iota(jnp.int32, (nr, nr), 0)
    rj = lax.broadcasted_iota(jnp.int32, (nr, nr), 1)
    lower_strict = (rj < ri).astype(BF16)

    def prefix(mask):
        w = _dot(mask.astype(BF16), upper_incl)
        tot = jnp.broadcast_to(w[:, LANES - 1:LANES], (nr, LANES))
        off = _dot(lower_strict, tot.astype(BF16))
        return w, tot, off

    w_eq, _, off_eq = prefix(eq)
    sel = gt | (eq & (off_eq + w_eq <= need))
    w, tot, off = prefix(sel)
    base = (e_id * cap).astype(F32)
    pos_ref[0] = jnp.where(sel, base + off + w - 1.0, -1.0)
    off_ref[0] = off

    off_excl = off[:, 0:1]
    off_incl = off_excl + tot[:, 0:1]
    w_t = w.T.astype(BF16)
    a_t = a.T
    a_hi_t = a_t.astype(BF16)
    r1 = a_t - a_hi_t.astype(F32)
    a_mid_t = r1.astype(BF16)
    a_lo_t = (r1 - a_mid_t.astype(F32)).astype(BF16)
    st = idx_ref.shape[2]
    r_col = lax.broadcasted_iota(jnp.int32, (nr, st), 0).astype(F32)
    l_col = lax.broadcasted_iota(jnp.int32, (LANES, st), 0).astype(F32)

    def slot_tile(t, carry):
        j = (t * st + lax.broadcasted_iota(jnp.int32, (1, st), 1)).astype(F32)
        hit = (off_excl <= j) & (j < off_incl)
        hit_b = hit.astype(BF16)
        k = j - jnp.sum(jnp.where(hit, off_excl, 0.0), axis=0, keepdims=True)
        row = jnp.sum(jnp.where(hit, r_col, 0.0), axis=0, keepdims=True)
        wsel = _dot(w_t, hit_b)
        lane = jnp.sum((wsel <= k).astype(F32), axis=0, keepdims=True)
        asel = _dot(a_hi_t, hit_b) + _dot(a_mid_t, hit_b) + _dot(a_lo_t, hit_b)
        gate = jnp.sum(jnp.where(l_col == lane, asel, 0.0), axis=0, keepdims=True)
        idx_ref[0, pl.ds(t, 1), :] = (row * LANES + lane).astype(jnp.int32)
        gate_ref[0, pl.ds(t, 1), :] = gate
        return carry

    lax.fori_loop(0, cap // st, slot_tile, 0)


ROUTE_SLOT_TILE = 512


def _route(aff_t, cap):
    n = aff_t.shape[1]
    nr = n // LANES
    st = min(ROUTE_SLOT_TILE, cap)
    a3 = aff_t.reshape(N_EXPERTS, nr, LANES)
    blk = lambda r, c=LANES: pl.BlockSpec((1, r, c), lambda e: (e, 0, 0))
    return pl.pallas_call(
        functools.partial(_route_body, cap=cap),
        grid=(N_EXPERTS,),
        in_specs=[blk(nr)],
        out_specs=[blk(cap // st, st), blk(cap // st, st), blk(nr), blk(nr)],
        out_shape=[jax.ShapeDtypeStruct((N_EXPERTS, cap // st, st), jnp.int32),
                   jax.ShapeDtypeStruct((N_EXPERTS, cap // st, st), F32),
                   jax.ShapeDtypeStruct((N_EXPERTS, nr, LANES), F32),
                   jax.ShapeDtypeStruct((N_EXPERTS, nr, LANES), F32)],
        compiler_params=_cparams(("parallel",)),
        name="ec_route",
    )(a3)


FFN_TC = 512
FFN_FCHUNK = 512


def _ffn_body(idx_cur, idx_nxt, x_hbm, g_ref, gate_ref, wg_ref, wu_ref, wd_ref, y_ref,
              xbuf, xe_sc, sem):
    tc = xbuf.shape[0]
    step = pl.program_id(0) * pl.num_programs(1) + pl.program_id(1)
    total = pl.num_programs(0) * pl.num_programs(1)

    def row_copy(idx, r):
        return pltpu.make_async_copy(x_hbm.at[pl.ds(idx[0, 0, r], 1)], xbuf.at[pl.ds(r, 1)],
                                     sem.at[0])

    def wait_rows():
        pltpu.make_async_copy(x_hbm.at[pl.ds(0, tc)], xbuf, sem.at[0]).wait()

    @pl.when(step == 0)
    def _():
        def one(r, c):
            row_copy(idx_cur, r).start()
            return c
        lax.fori_loop(0, tc, one, 0)

    wait_rows()
    xe_sc[...] = _rms(xbuf[...], g_ref[...]).astype(BF16)
    acc = jnp.zeros((tc, D_MODEL), F32)
    nchunk = D_EXPERT // FFN_FCHUNK
    rows_per_chunk = tc // nchunk
    for c in range(nchunk):
        for r in range(c * rows_per_chunk, (c + 1) * rows_per_chunk):
            row_copy(idx_nxt, r).start(priority=r % 2)
        fs = slice(c * FFN_FCHUNK, (c + 1) * FFN_FCHUNK)
        xe = xe_sc[...]
        hg = _dot(xe, wg_ref[0, :, fs])
        hu = _dot(xe, wu_ref[0, :, fs])
        hid = (hg * jax.nn.sigmoid(hg)) * hu
        acc = acc + _dot(hid.astype(BF16), wd_ref[0, fs, :])
    y_ref[...] = (acc * gate_ref[0]).astype(BF16)

    @pl.when(step == total - 1)
    def _():
        wait_rows()


def _expert_ffn(x, g, idx, gate, wg, wu, wd, cap):
    tc = min(FFN_TC, cap)
    nt = cap // tc
    ntiles = N_EXPERTS * nt
    idx3 = idx.reshape(ntiles, 1, tc)
    gate3 = gate.reshape(N_EXPERTS, cap, 1)
    smem = lambda m: pl.BlockSpec((1, 1, tc), m, memory_space=pltpu.SMEM)
    wspec = lambda a, b: pl.BlockSpec((1, a, b), lambda e, i: (e, 0, 0))
    return pl.pallas_call(
        _ffn_body,
        grid=(N_EXPERTS, nt),
        in_specs=[smem(lambda e, i: (e * nt + i, 0, 0)),
                  smem(lambda e, i: (jnp.minimum(e * nt + i + 1, ntiles - 1), 0, 0)),
                  pl.BlockSpec(memory_space=pl.ANY),
                  pl.BlockSpec((1, D_MODEL), lambda e, i: (0, 0)),
                  pl.BlockSpec((1, tc, 1), lambda e, i: (e, i, 0)),
                  wspec(D_MODEL, D_EXPERT), wspec(D_MODEL, D_EXPERT), wspec(D_EXPERT, D_MODEL)],
        out_specs=pl.BlockSpec((tc, D_MODEL), lambda e, i: (e * nt + i, 0)),
        out_shape=jax.ShapeDtypeStruct((N_EXPERTS * cap, D_MODEL), BF16),
        scratch_shapes=[pltpu.VMEM((tc, D_MODEL), F32), pltpu.VMEM((tc, D_MODEL), BF16),
                        pltpu.SemaphoreType.DMA((1,))],
        compiler_params=_cparams(("arbitrary", "arbitrary")),
        name="expert_ffn",
    )(idx3, idx3, x, g, gate3, wg, wu, wd)


COMB_TB = 512
COMB_CW = 128
COMB_ALIGN = 16


def _combine_body(boff_ref, x_ref, pos_ref, y_hbm, gfin_ref, o_ref, win, xwin, sem, xsem,
                  *, cap, final_norm):
    b = pl.program_id(0)
    total = N_EXPERTS * cap
    tb = x_ref.shape[0]
    cw = COMB_CW

    slot = b % 2

    def wstart(e, blk):
        s = jnp.bitwise_and(e * cap + boff_ref[e, blk], -COMB_ALIGN)
        return pl.multiple_of(jnp.minimum(s, total - cw), COMB_ALIGN)

    def window_copy(e, start, dst_slot):
        return pltpu.make_async_copy(y_hbm.at[pl.ds(start, cw)],
                                     win.at[dst_slot, pl.ds(e * cw, cw)], sem.at[dst_slot, e])

    @pl.when(b == 0)
    def _():
        for e in range(N_EXPERTS):
            window_copy(e, wstart(e, 0), 0).start()

    @pl.when(b + 1 < pl.num_programs(0))
    def _():
        for e in range(N_EXPERTS):
            window_copy(e, wstart(e, b + 1), 1 - slot).start()

    starts = [wstart(e, b) for e in range(N_EXPERTS)]
    lane = lax.broadcasted_iota(jnp.int32, (tb, cw), 1).astype(F32)
    pos = pos_ref[...]
    per_group = LANES // cw
    glane = lax.broadcasted_iota(jnp.int32, (tb, LANES), 1)
    glane_f = glane.astype(F32)
    pieces = []
    for g in range(N_EXPERTS // per_group):
        target = None
        for u in reversed(range(per_group)):
            e = g * per_group + u
            d = pos[:, e:e + 1] - starts[e].astype(F32) + float(u * cw)
            target = d if target is None else jnp.where(glane < (u + 1) * cw, d, target)
        pieces.append((target == glane_f).astype(BF16))
    onehot = jnp.concatenate(pieces, axis=1)
    for e in range(N_EXPERTS):
        window_copy(e, starts[e], slot).wait()
    o_ref[...] = x_ref[...] + _dot(onehot, win[slot])

    for e in range(N_EXPERTS):
        end = e * cap + boff_ref[e, b + 1]
        extra = lax.shift_right_logical(jnp.maximum(end - starts[e] - 1, 0), int(math.log2(cw)))

        def more(c, carry, e=e):
            lo = starts[e] + c * cw
            s = pl.multiple_of(jnp.minimum(lo, total - cw), COMB_ALIGN)
            cp = pltpu.make_async_copy(y_hbm.at[pl.ds(s, cw)], xwin, xsem.at[0])
            cp.start()
            cp.wait()
            pe = pos[:, e:e + 1]
            oh = ((pe - s.astype(F32) == lane) & (pe >= lo.astype(F32))).astype(BF16)
            o_ref[...] += _dot(oh, xwin[...])
            return carry

        lax.fori_loop(1, extra + 1, more, 0)

    if final_norm:
        o_ref[...] = _rms(o_ref[...], gfin_ref[...])


def _combine(x, pos_t, boff, y, gfin, cap, final_norm):
    n = x.shape[0]
    tb = COMB_TB
    grid_spec = pltpu.PrefetchScalarGridSpec(
        num_scalar_prefetch=1,
        grid=(n // tb,),
        in_specs=[pl.BlockSpec((tb, D_MODEL), lambda b, s: (b, 0)),
                  pl.BlockSpec((tb, N_EXPERTS), lambda b, s: (b, 0)),
                  pl.BlockSpec(memory_space=pl.ANY),
                  pl.BlockSpec((1, D_MODEL), lambda b, s: (0, 0))],
        out_specs=pl.BlockSpec((tb, D_MODEL), lambda b, s: (b, 0)),
        scratch_shapes=[pltpu.VMEM((2, N_EXPERTS * COMB_CW, D_MODEL), BF16),
                        pltpu.VMEM((COMB_CW, D_MODEL), BF16),
                        pltpu.SemaphoreType.DMA((2, N_EXPERTS)),
                        pltpu.SemaphoreType.DMA((1,))],
    )
    return pl.pallas_call(
        functools.partial(_combine_body, cap=cap, final_norm=final_norm),
        grid_spec=grid_spec,
        out_shape=jax.ShapeDtypeStruct((n, D_MODEL), F32),
        compiler_params=_cparams(("arbitrary",)),
        name="ec_combine",
    )(boff, x, pos_t, y, gfin)


def _moe(xn, aff_t, ffn_norm, wg, wu, wd, gfin, final_norm):
    n = xn.shape[0]
    cap = EC_CAPACITY_FACTOR * n // N_EXPERTS
    idx, gate, pos, off = _route(aff_t, cap)
    y = _expert_ffn(xn, ffn_norm, idx, gate, wg, wu, wd, cap)
    pos_t = pos.reshape(N_EXPERTS, n).T
    rows_per_blk = COMB_TB // LANES
    boff = off[:, ::rows_per_blk, 0].astype(jnp.int32)
    boff = jnp.concatenate([boff, jnp.full((N_EXPERTS, 1), cap, jnp.int32)], axis=1)
    return _combine(xn, pos_t, boff, y, gfin, cap, final_norm)


def _expand_kv_columns(w_qkv):
    qd = SW_HEADS * HEAD_DIM
    kd = SW_KV_HEADS * HEAD_DIM
    head_of = jnp.arange(SW_HEADS) // (SW_HEADS // SW_KV_HEADS)
    cols = (head_of[:, None] * HEAD_DIM + jnp.arange(HEAD_DIM)[None, :]).reshape(-1)
    return jnp.concatenate([w_qkv[:, :qd], w_qkv[:, qd:qd + kd][:, cols],
                            w_qkv[:, qd + kd:][:, cols]], axis=1)


def _run_group(x3, layers, final_norm_g):
    bsz, seq_len, _ = x3.shape
    n = bsz * seq_len
    x = x3.reshape(n, D_MODEL)
    cos_t, sin_t = _rope_tables(seq_len)
    gfin = final_norm_g.reshape(1, D_MODEL).astype(F32)
    for i, lp in enumerate(layers):
        kind = MIXER_PATTERN[i % len(MIXER_PATTERN)]
        q, k, v = _qkv_proj(x, lp["mix_norm"], lp["w_qkv"], cos_t, sin_t, seq_len, kind != "nat")
        if kind == "nat":
            o = _nat_attention(q, k, v, lp["nat_bias"], bsz, seq_len)
        elif kind == "swa":
            o = _swa_attention(q, k, v, lp["sink"], bsz, seq_len)
        else:
            lambda_init = 0.8 - 0.6 * math.exp(-0.3 * i)
            o = _diff_attention(q, k, v, lp["lam4"], lp["subln"], bsz, seq_len, lambda_init)
        xn, aff_t = _oproj_router(o, x, lp["w_o"], lp["ffn_norm"], lp["w_router_t"])
        x = _moe(xn, aff_t, lp["ffn_norm"], lp["w_gate"], lp["w_up"], lp["w_down"],
                 gfin, i == len(layers) - 1)
    return x.reshape(bsz, seq_len, D_MODEL)


def _prepare_layer(i, lp):
    kind = MIXER_PATTERN[i % len(MIXER_PATTERN)]
    out = dict(
        mix_norm=lp["mix_norm"].reshape(1, D_MODEL).astype(F32),
        ffn_norm=lp["ffn_norm"].reshape(1, D_MODEL).astype(F32),
        w_o=lp["w_o"].astype(BF16),
        w_router_t=lp["w_router"].T.astype(BF16),
        w_gate=lp["w_gate"].astype(BF16), w_up=lp["w_up"].astype(BF16),
        w_down=lp["w_down"].astype(BF16),
    )
    if kind == "swa":
        out["w_qkv"] = _expand_kv_columns(lp["w_qkv"]).astype(BF16)
        out["sink"] = lp["sink"]
    else:
        out["w_qkv"] = lp["w_qkv"].astype(BF16)
    if kind == "nat":
        out["nat_bias"] = _nat_bias_table(lp["rpb"])
    if kind == "diff":
        out["lam4"] = jnp.stack([lp["lambda_q1"], lp["lambda_k1"], lp["lambda_q2"],
                                 lp["lambda_k2"]]).astype(F32)
        out["subln"] = lp["subln"].reshape(1, LANES).astype(F32)
    return out


def kernel(x_prompt, x_sample, l0_mix_norm, l0_w_qkv, l0_rpb, l0_w_o, l0_ffn_norm, l0_w_router, l0_w_gate, l0_w_up, l0_w_down, l1_mix_norm, l1_w_qkv, l1_sink, l1_w_o, l1_ffn_norm, l1_w_router, l1_w_gate, l1_w_up, l1_w_down, l2_mix_norm, l2_w_qkv, l2_lambda_q1, l2_lambda_k1, l2_lambda_q2, l2_lambda_k2, l2_subln, l2_w_o, l2_ffn_norm, l2_w_router, l2_w_gate, l2_w_up, l2_w_down, l3_mix_norm, l3_w_qkv, l3_rpb, l3_w_o, l3_ffn_norm, l3_w_router, l3_w_gate, l3_w_up, l3_w_down, final_norm):
    layers = [
        dict(mix_norm=l0_mix_norm, w_qkv=l0_w_qkv, rpb=l0_rpb, w_o=l0_w_o, ffn_norm=l0_ffn_norm,
             w_router=l0_w_router, w_gate=l0_w_gate, w_up=l0_w_up, w_down=l0_w_down),
        dict(mix_norm=l1_mix_norm, w_qkv=l1_w_qkv, sink=l1_sink, w_o=l1_w_o, ffn_norm=l1_ffn_norm,
             w_router=l1_w_router, w_gate=l1_w_gate, w_up=l1_w_up, w_down=l1_w_down),
        dict(mix_norm=l2_mix_norm, w_qkv=l2_w_qkv, lambda_q1=l2_lambda_q1, lambda_k1=l2_lambda_k1,
             lambda_q2=l2_lambda_q2, lambda_k2=l2_lambda_k2, subln=l2_subln, w_o=l2_w_o,
             ffn_norm=l2_ffn_norm, w_router=l2_w_router, w_gate=l2_w_gate, w_up=l2_w_up,
             w_down=l2_w_down),
        dict(mix_norm=l3_mix_norm, w_qkv=l3_w_qkv, rpb=l3_rpb, w_o=l3_w_o, ffn_norm=l3_ffn_norm,
             w_router=l3_w_router, w_gate=l3_w_gate, w_up=l3_w_up, w_down=l3_w_down),
    ]
    layers = [_prepare_layer(i, lp) for i, lp in enumerate(layers)]
    y_prompt = _run_group(x_prompt, layers, final_norm)
    y_sample = _run_group(x_sample, layers, final_norm)
    return (y_prompt, y_sample)
```

```python
import functools
import math

import jax
import jax.numpy as jnp
import numpy as np
from jax import lax
from jax.experimental import pallas as pl
from jax.experimental.pallas import tpu as pltpu

D_MODEL = 1024
DEPTH = 4
GRID_W = 64
NA_HEADS = 16
NA_WIN_H = 8
NA_WIN_W = 16
SW_HEADS = 16
SW_KV_HEADS = 4
SW_BLOCK = 128
DIFF_HEADS = 8
HEAD_DIM = 64
N_EXPERTS = 16
EC_CAPACITY_FACTOR = 2
D_EXPERT = 2048
ROPE_THETA = 10000.0
NORM_EPS = 1e-6
MIXER_PATTERN = ("nat", "swa", "diff")

LANES = 128
VMEM_LIMIT = 56 * 1024 * 1024

BF16 = jnp.bfloat16
F32 = jnp.float32
NEG_INF = float("-inf")


def _cparams(sem):
    return pltpu.CompilerParams(dimension_semantics=sem, vmem_limit_bytes=VMEM_LIMIT)


def _dot(a, b):
    return jnp.dot(a, b, preferred_element_type=F32)


def _dot_nt(a, b):
    return lax.dot_general(a, b, (((1,), (1,)), ((), ())), preferred_element_type=F32)


def _dot_tn(a, b):
    return lax.dot_general(a, b, (((0,), (0,)), ((), ())), preferred_element_type=F32)


def _rms(x, g):
    return x * lax.rsqrt(jnp.mean(x * x, axis=-1, keepdims=True) + NORM_EPS) * g


PROJ_TM = 512


def _qkv_body(x_ref, g_ref, w_ref, cos_ref, sin_ref, q_ref, k_ref, v_ref, *, use_rope):
    h = _rms(x_ref[...], g_ref[...])
    y = _dot(h.astype(BF16), w_ref[...])
    tm = y.shape[0]
    lane = lax.broadcasted_iota(jnp.int32, (tm, LANES), 1)
    first_half = (lane % HEAD_DIM) < (HEAD_DIM // 2)
    if use_rope:
        cos = cos_ref[...]
        sin = sin_ref[...]
    outs = (q_ref, k_ref, v_ref)
    for part in range(3):
        for c in range(D_MODEL // LANES):
            col = part * D_MODEL + c * LANES
            blk = y[:, col:col + LANES]
            if use_rope and part < 2:
                rot = jnp.where(first_half,
                                pltpu.roll(blk, LANES - HEAD_DIM // 2, 1),
                                pltpu.roll(blk, HEAD_DIM // 2, 1))
                blk = blk * cos + rot * sin
            if part == 0:
                blk = blk * (HEAD_DIM ** -0.5)
            outs[part][:, c * LANES:(c + 1) * LANES] = blk.astype(BF16)


def _qkv_proj(x, g, w, cos, sin, seq_len, use_rope):
    n = x.shape[0]
    tm = PROJ_TM
    tblocks = seq_len // tm
    out = jax.ShapeDtypeStruct((n, D_MODEL), BF16)
    row = lambda i: (i, 0)
    return pl.pallas_call(
        functools.partial(_qkv_body, use_rope=use_rope),
        grid=(n // tm,),
        in_specs=[
            pl.BlockSpec((tm, D_MODEL), row),
            pl.BlockSpec((1, D_MODEL), lambda i: (0, 0)),
            pl.BlockSpec((D_MODEL, 3 * D_MODEL), lambda i: (0, 0)),
            pl.BlockSpec((tm, LANES), lambda i: (i % tblocks, 0)),
            pl.BlockSpec((tm, LANES), lambda i: (i % tblocks, 0)),
        ],
        out_specs=[pl.BlockSpec((tm, D_MODEL), row)] * 3,
        out_shape=[out, out, out],
        compiler_params=_cparams(("parallel",)),
        name="qkv_proj",
    )(x, g, w, cos, sin)


def _rope_tables(seq_len):
    inv = ROPE_THETA ** (-jnp.arange(0, HEAD_DIM, 2, dtype=F32) / HEAD_DIM)
    ang = jnp.arange(seq_len, dtype=F32)[:, None] * inv[None, :]
    cos = jnp.cos(ang)
    sin = jnp.sin(ang)
    cos_t = jnp.tile(jnp.concatenate([cos, cos], axis=1), (1, LANES // HEAD_DIM))
    sin_t = jnp.tile(jnp.concatenate([-sin, sin], axis=1), (1, LANES // HEAD_DIM))
    return cos_t, sin_t


def _pair_split(qp):
    lane = lax.broadcasted_iota(jnp.int32, qp.shape, 1)
    zero = jnp.zeros_like(qp)
    return jnp.concatenate([jnp.where(lane < HEAD_DIM, qp, zero),
                            jnp.where(lane >= HEAD_DIM, qp, zero)], axis=0)


def _pair_merge(o, m):
    lane = lax.broadcasted_iota(jnp.int32, (m, LANES), 1)
    return jnp.where(lane < HEAD_DIM, o[:m], o[m:])


NAT_ROWS_PER_STEP = 4
NAT_TQ = NAT_ROWS_PER_STEP * GRID_W
NAT_WIN_BLOCKS = 3
NAT_PAIRS_PER_STEP = 4


def _nat_body(q_ref, k0_ref, k1_ref, k2_ref, v0_ref, v1_ref, v2_ref, b_ref, o_ref):
    krefs = (k0_ref, k1_ref, k2_ref)
    vrefs = (v0_ref, v1_ref, v2_ref)
    for p in range(NAT_PAIRS_PER_STEP):
        cs = slice(p * LANES, (p + 1) * LANES)
        qs = _pair_split(q_ref[:, cs])
        s = [_dot_nt(qs, krefs[j][:, cs]) + b_ref[0, p, :, j * NAT_TQ:(j + 1) * NAT_TQ]
             for j in range(NAT_WIN_BLOCKS)]
        m = jnp.maximum(jnp.maximum(jnp.max(s[0], axis=-1, keepdims=True),
                                    jnp.max(s[1], axis=-1, keepdims=True)),
                        jnp.max(s[2], axis=-1, keepdims=True))
        e = [jnp.exp(sj - m) for sj in s]
        l = (jnp.sum(e[0], axis=-1, keepdims=True) + jnp.sum(e[1], axis=-1, keepdims=True)
             + jnp.sum(e[2], axis=-1, keepdims=True))
        r = 1.0 / l
        o = (_dot((e[0] * r).astype(BF16), vrefs[0][:, cs])
             + _dot((e[1] * r).astype(BF16), vrefs[1][:, cs])
             + _dot((e[2] * r).astype(BF16), vrefs[2][:, cs]))
        o_ref[:, cs] = _pair_merge(o, NAT_TQ).astype(BF16)


def _nat_bias_table(rpb):
    cols = jnp.arange(GRID_W)
    col_start = jnp.clip(cols - NA_WIN_W // 2, 0, GRID_W - NA_WIN_W)
    col_in = (cols[None, :] >= col_start[:, None]) & (cols[None, :] < col_start[:, None] + NA_WIN_W)
    dc_idx = jnp.clip(cols[None, :] - cols[:, None] + NA_WIN_W - 1, 0, 2 * NA_WIN_W - 2)
    tz = jnp.where(col_in[None, None], rpb.astype(F32)[:, :, dc_idx], NEG_INF)
    n_dr = 2 * NA_WIN_H - 1
    tz = jnp.concatenate([tz, jnp.full((NA_HEADS, 1, GRID_W, GRID_W), NEG_INF, F32)], axis=1)
    win_rows = NAT_WIN_BLOCKS * NAT_ROWS_PER_STEP
    half = NA_WIN_H // 2
    d_idx = np.full((3, NAT_ROWS_PER_STEP, win_rows), n_dr, np.int32)
    for case in range(3):
        for a in range(NAT_ROWS_PER_STEP):
            j0, d0 = ((0, NA_WIN_H - 1 - a), (a, NA_WIN_H - 1 - half), (half, NA_WIN_H - 1 - half - a))[case]
            for jj in range(NA_WIN_H):
                d_idx[case, a, j0 + jj] = d0 + jj
    t = tz[:, d_idx]
    t = jnp.transpose(t, (1, 0, 2, 4, 3, 5))
    return t.reshape(3, NA_HEADS // 2, 2 * NAT_TQ, win_rows * GRID_W)


def _nat_attention(q, k, v, bias, bsz, seq_len):
    n = q.shape[0]
    rows = seq_len // GRID_W
    nb = rows // NAT_ROWS_PER_STEP
    assert rows % NAT_ROWS_PER_STEP == 0 and nb >= NAT_WIN_BLOCKS
    pp = NAT_PAIRS_PER_STEP
    width = pp * LANES

    def win(j):
        return lambda hp, b, i: (b * nb + jnp.clip(i - 1, 0, nb - NAT_WIN_BLOCKS) + j, hp)

    def bias_map(hp, b, i):
        case = jnp.where(i == 0, 0, jnp.where(i == nb - 1, 2, 1))
        return (case, hp, 0, 0)

    qmap = lambda hp, b, i: (b * nb + i, hp)
    blk = lambda m: pl.BlockSpec((NAT_TQ, width), m)
    return pl.pallas_call(
        _nat_body,
        grid=(NA_HEADS // 2 // pp, bsz, nb),
        in_specs=[blk(qmap), blk(win(0)), blk(win(1)), blk(win(2)),
                  blk(win(0)), blk(win(1)), blk(win(2)),
                  pl.BlockSpec((1, pp, 2 * NAT_TQ, NAT_WIN_BLOCKS * NAT_TQ), bias_map)],
        out_specs=blk(qmap),
        out_shape=jax.ShapeDtypeStruct((n, D_MODEL), BF16),
        compiler_params=_cparams(("parallel", "parallel", "parallel")),
        name="nat_attention",
    )(q, k, k, k, v, v, v, bias)


def _swa_body(sink_ref, q_ref, k0_ref, k1_ref, k2_ref, v0_ref, v1_ref, v2_ref, o_ref, *, nblk):
    i = pl.program_id(1)
    tb = SW_BLOCK
    group = SW_HEADS // SW_KV_HEADS
    qq = lax.broadcasted_iota(jnp.int32, (group * tb, 3 * tb), 0) % tb
    kk = lax.broadcasted_iota(jnp.int32, (group * tb, 3 * tb), 1) - tb
    valid = ((jnp.abs(kk - qq) <= tb) & ((kk >= 0) | (i > 0)) & ((kk < tb) | (i < nblk - 1)))
    row = lax.broadcasted_iota(jnp.int32, (group * tb, 1), 0)
    for u in range(SW_KV_HEADS):
        pairs = [slice((2 * u + t) * LANES, (2 * u + t + 1) * LANES) for t in range(group // 2)]
        qs = jnp.concatenate([_pair_split(q_ref[:, cs]) for cs in pairs], axis=0)
        kc = jnp.concatenate([k0_ref[:, pairs[0]], k1_ref[:, pairs[0]], k2_ref[:, pairs[0]]], axis=0)
        vc = jnp.concatenate([v0_ref[:, pairs[0]], v1_ref[:, pairs[0]], v2_ref[:, pairs[0]]], axis=0)
        s = jnp.where(valid, _dot_nt(qs, kc), NEG_INF)
        sink = sink_ref[group * u + group - 1]
        for h in reversed(range(group - 1)):
            sink = jnp.where(row < (h + 1) * tb, sink_ref[group * u + h], sink)
        m = jnp.maximum(jnp.max(s, axis=-1, keepdims=True), sink)
        e = jnp.exp(s - m)
        l = jnp.sum(e, axis=-1, keepdims=True) + jnp.exp(sink - m)
        o = _dot((e / l).astype(BF16), vc)
        for t, cs in enumerate(pairs):
            o_ref[:, cs] = _pair_merge(o[2 * t * tb:(2 * t + 2) * tb], tb).astype(BF16)


def _swa_attention(q, k, v, sink, bsz, seq_len):
    n = q.shape[0]
    nblk = seq_len // SW_BLOCK
    cur = lambda b, i: (b * nblk + i, 0)
    prev = lambda b, i: (b * nblk + jnp.maximum(i - 1, 0), 0)
    nxt = lambda b, i: (b * nblk + jnp.minimum(i + 1, nblk - 1), 0)
    blk = lambda m: pl.BlockSpec((SW_BLOCK, D_MODEL), m)
    return pl.pallas_call(
        functools.partial(_swa_body, nblk=nblk),
        grid=(bsz, nblk),
        in_specs=[pl.BlockSpec(memory_space=pltpu.SMEM),
                  blk(cur), blk(prev), blk(cur), blk(nxt), blk(prev), blk(cur), blk(nxt)],
        out_specs=blk(cur),
        out_shape=jax.ShapeDtypeStruct((n, D_MODEL), BF16),
        compiler_params=_cparams(("parallel", "parallel")),
        name="swa_attention",
    )(sink.astype(F32), q, k, k, k, v, v, v)


DIFF_TQ = 1024
DIFF_TK = 2048
DIFF_ROW_CHUNKS = 4


def _diff_body(q_ref, k_ref, v_ref, lam_ref, subln_ref, o_ref, qs_sc, m_sc, l_sc, acc_sc,
               *, lambda_init):
    kv = pl.program_id(3)
    tq = q_ref.shape[0]
    tk = k_ref.shape[0]

    @pl.when(kv == 0)
    def _():
        qs_sc[...] = _pair_split(q_ref[...])
        m_sc[...] = jnp.full(m_sc.shape, NEG_INF, F32)
        l_sc[...] = jnp.zeros(l_sc.shape, F32)
        acc_sc[...] = jnp.zeros(acc_sc.shape, F32)

    k = k_ref[...]
    v = v_ref[...]
    rows = 2 * tq // DIFF_ROW_CHUNKS
    for c in range(DIFF_ROW_CHUNKS):
        rs = slice(c * rows, (c + 1) * rows)
        s = _dot_nt(qs_sc[rs, :], k)
        m_prev = m_sc[rs, :]
        m_new = jnp.maximum(m_prev, jnp.max(s, axis=-1, keepdims=True))
        alpha = jnp.exp(m_prev - m_new)
        p = jnp.exp(s - jnp.concatenate([m_new] * (tk // LANES), axis=1))
        l_sc[rs, :] = alpha * l_sc[rs, :] + jnp.sum(p, axis=-1, keepdims=True)
        acc_sc[rs, :] = alpha * acc_sc[rs, :] + _dot(p.astype(BF16), v)
        m_sc[rs, :] = m_new

    @pl.when(kv == pl.num_programs(3) - 1)
    def _():
        o12 = acc_sc[...] / l_sc[...]
        lam4 = lam_ref[...]
        lam = (jnp.exp(jnp.sum(lam4[0:1] * lam4[1:2], axis=-1, keepdims=True))
               - jnp.exp(jnp.sum(lam4[2:3] * lam4[3:4], axis=-1, keepdims=True)) + lambda_init)
        o = o12[:tq] - lam * o12[tq:]
        o = _rms(o, subln_ref[...]) * (1.0 - lambda_init)
        o_ref[...] = o.astype(BF16)


def _diff_attention(q, k, v, lam4, subln, bsz, seq_len, lambda_init):
    n = q.shape[0]
    tq = min(DIFF_TQ, seq_len)
    tk = min(DIFF_TK, seq_len)
    nq = seq_len // tq
    nk = seq_len // tk
    return pl.pallas_call(
        functools.partial(_diff_body, lambda_init=lambda_init),
        grid=(bsz, DIFF_HEADS, nq, nk),
        in_specs=[
            pl.BlockSpec((tq, LANES), lambda b, h, qi, ki: (b * nq + qi, h)),
            pl.BlockSpec((tk, LANES), lambda b, h, qi, ki: (b * nk + ki, h)),
            pl.BlockSpec((tk, LANES), lambda b, h, qi, ki: (b * nk + ki, h)),
            pl.BlockSpec((4, HEAD_DIM), lambda b, h, qi, ki: (0, 0)),
            pl.BlockSpec((1, LANES), lambda b, h, qi, ki: (0, 0)),
        ],
        out_specs=pl.BlockSpec((tq, LANES), lambda b, h, qi, ki: (b * nq + qi, h)),
        out_shape=jax.ShapeDtypeStruct((n, D_MODEL), BF16),
        scratch_shapes=[pltpu.VMEM((2 * tq, LANES), BF16), pltpu.VMEM((2 * tq, LANES), F32),
                        pltpu.VMEM((2 * tq, LANES), F32), pltpu.VMEM((2 * tq, LANES), F32)],
        compiler_params=_cparams(("parallel", "parallel", "parallel", "arbitrary")),
        name="diff_attention",
    )(q, k, v, lam4, subln)


def _oproj_body(o_ref, x_ref, wo_ref, g_ref, wr_ref, xn_ref, aff_ref):
    xn = x_ref[...] + _dot(o_ref[...], wo_ref[...])
    xn_ref[...] = xn
    h = _rms(xn, g_ref[...]).astype(BF16)
    logits = _dot_nt(wr_ref[...], h)
    m = jnp.max(logits, axis=0, keepdims=True)
    e = jnp.exp(logits - m)
    aff_ref[...] = e / jnp.sum(e, axis=0, keepdims=True)


OPROJ_TM = 512


def _oproj_router(o, x, wo, g, wr_t):
    n = x.shape[0]
    tm = OPROJ_TM
    row = lambda i: (i, 0)
    const = lambda i: (0, 0)
    return pl.pallas_call(
        _oproj_body,
        grid=(n // tm,),
        in_specs=[pl.BlockSpec((tm, D_MODEL), row), pl.BlockSpec((tm, D_MODEL), row),
                  pl.BlockSpec((D_MODEL, D_MODEL), const), pl.BlockSpec((1, D_MODEL), const),
                  pl.BlockSpec((N_EXPERTS, D_MODEL), const)],
        out_specs=[pl.BlockSpec((tm, D_MODEL), row), pl.BlockSpec((N_EXPERTS, tm), lambda i: (0, i))],
        out_shape=[jax.ShapeDtypeStruct((n, D_MODEL), F32),
                   jax.ShapeDtypeStruct((N_EXPERTS, n), F32)],
        compiler_params=_cparams(("parallel",)),
        name="oproj_router",
    )(o, x, wo, g, wr_t)


def _route_body(a_ref, idx_ref, gate_ref, pos_ref, off_ref, *, cap):
    e_id = pl.program_id(0)
    a = a_ref[0]
    nr = a.shape[0]
    bits = pltpu.bitcast(a, jnp.int32)

    def count(mask):
        c = jnp.sum(mask.astype(F32), axis=1, keepdims=True)
        return jnp.sum(c, axis=0, keepdims=True)

    def search(step, thr):
        cand = thr | lax.shift_left(jnp.int32(1), 30 - step)
        return jnp.where(count(bits >= cand) >= cap, cand, thr)

    thr = lax.fori_loop(0, 31, search, jnp.zeros((1, 1), jnp.int32))
    gt = bits > thr
    eq = bits == thr
    need = cap - count(gt)

    li = lax.broadcasted_iota(jnp.int32, (LANES, LANES), 0)
    lj = lax.broadcasted_iota(jnp.int32, (LANES, LANES), 1)
    upper_incl = (li <= lj).astype(BF16)
    ri = lax.broadcasted_iota(jnp.int32, (nr, nr), 0)
    rj = lax.broadcasted_iota(jnp.int32, (nr, nr), 1)
    lower_strict = (rj < ri).astype(BF16)

    def prefix(mask):
        w = _dot(mask.astype(BF16), upper_incl)
        tot = jnp.broadcast_to(w[:, LANES - 1:LANES], (nr, LANES))
        off = _dot(lower_strict, tot.astype(BF16))
        return w, tot, off

    w_eq, _, off_eq = prefix(eq)
    sel = gt | (eq & (off_eq + w_eq <= need))
    w, tot, off = prefix(sel)
    base = (e_id * cap).astype(F32)
    pos_ref[0] = jnp.where(sel, base + off + w - 1.0, -1.0)
    off_ref[0] = off

    off_excl = off[:, 0:1]
    off_incl = off_excl + tot[:, 0:1]
    w_t = w.T.astype(BF16)
    a_t = a.T
    a_hi_t = a_t.astype(BF16)
    r1 = a_t - a_hi_t.astype(F32)
    a_mid_t = r1.astype(BF16)
    a_lo_t = (r1 - a_mid_t.astype(F32)).astype(BF16)
    st = idx_ref.shape[2]
    r_col = lax.broadcasted_iota(jnp.int32, (nr, st), 0).astype(F32)
    l_col = lax.broadcasted_iota(jnp.int32, (LANES, st), 0).astype(F32)

    def slot_tile(t, carry):
        j = (t * st + lax.broadcasted_iota(jnp.int32, (1, st), 1)).astype(F32)
        hit = (off_excl <= j) & (j < off_incl)
        hit_b = hit.astype(BF16)
        k = j - jnp.sum(jnp.where(hit, off_excl, 0.0), axis=0, keepdims=True)
        row = jnp.sum(jnp.where(hit, r_col, 0.0), axis=0, keepdims=True)
        wsel = _dot(w_t, hit_b)
        lane = jnp.sum((wsel <= k).astype(F32), axis=0, keepdims=True)
        asel = _dot(a_hi_t, hit_b) + _dot(a_mid_t, hit_b) + _dot(a_lo_t, hit_b)
        gate = jnp.sum(jnp.where(l_col == lane, asel, 0.0), axis=0, keepdims=True)
        idx_ref[0, pl.ds(t, 1), :] = (row * LANES + lane).astype(jnp.int32)
        gate_ref[0, pl.ds(t, 1), :] = gate
        return carry

    lax.fori_loop(0, cap // st, slot_tile, 0)


ROUTE_SLOT_TILE = 512


def _route(aff_t, cap):
    n = aff_t.shape[1]
    nr = n // LANES
    st = min(ROUTE_SLOT_TILE, cap)
    a3 = aff_t.reshape(N_EXPERTS, nr, LANES)
    blk = lambda r, c=LANES: pl.BlockSpec((1, r, c), lambda e: (e, 0, 0))
    return pl.pallas_call(
        functools.partial(_route_body, cap=cap),
        grid=(N_EXPERTS,),
        in_specs=[blk(nr)],
        out_specs=[blk(cap // st, st), blk(cap // st, st), blk(nr), blk(nr)],
        out_shape=[jax.ShapeDtypeStruct((N_EXPERTS, cap // st, st), jnp.int32),
                   jax.ShapeDtypeStruct((N_EXPERTS, cap // st, st), F32),
                   jax.ShapeDtypeStruct((N_EXPERTS, nr, LANES), F32),
                   jax.ShapeDtypeStruct((N_EXPERTS, nr, LANES), F32)],
        compiler_params=_cparams(("parallel",)),
        name="ec_route",
    )(a3)


FFN_TC = 512
FFN_FCHUNK = 512


def _ffn_body(idx_cur, idx_nxt, x_hbm, g_ref, gate_ref, wg_ref, wu_ref, wd_ref, y_ref,
              xbuf, xe_sc, sem):
    tc = xbuf.shape[0]
    step = pl.program_id(0) * pl.num_programs(1) + pl.program_id(1)
    total = pl.num_programs(0) * pl.num_programs(1)

    def row_copy(idx, r):
        return pltpu.make_async_copy(x_hbm.at[pl.ds(idx[0, 0, r], 1)], xbuf.at[pl.ds(r, 1)],
                                     sem.at[0])

    def wait_rows():
        pltpu.make_async_copy(x_hbm.at[pl.ds(0, tc)], xbuf, sem.at[0]).wait()

    @pl.when(step == 0)
    def _():
        def one(r, c):
            row_copy(idx_cur, r).start()
            return c
        lax.fori_loop(0, tc, one, 0)

    wait_rows()
    xe_sc[...] = _rms(xbuf[...], g_ref[...]).astype(BF16)
    acc = jnp.zeros((tc, D_MODEL), F32)
    nchunk = D_EXPERT // FFN_FCHUNK
    rows_per_chunk = tc // nchunk
    for c in range(nchunk):
        for r in range(c * rows_per_chunk, (c + 1) * rows_per_chunk):
            row_copy(idx_nxt, r).start(priority=r % 2)
        fs = slice(c * FFN_FCHUNK, (c + 1) * FFN_FCHUNK)
        xe = xe_sc[...]
        hg = _dot(xe, wg_ref[0, :, fs])
        hu = _dot(xe, wu_ref[0, :, fs])
        hid = (hg * jax.nn.sigmoid(hg)) * hu
        acc = acc + _dot(hid.astype(BF16), wd_ref[0, fs, :])
    y_ref[...] = (acc * gate_ref[0]).astype(BF16)

    @pl.when(step == total - 1)
    def _():
        wait_rows()


def _expert_ffn(x, g, idx, gate, wg, wu, wd, cap):
    tc = min(FFN_TC, cap)
    nt = cap // tc
    ntiles = N_EXPERTS * nt
    idx3 = idx.reshape(ntiles, 1, tc)
    gate3 = gate.reshape(N_EXPERTS, cap, 1)
    smem = lambda m: pl.BlockSpec((1, 1, tc), m, memory_space=pltpu.SMEM)
    wspec = lambda a, b: pl.BlockSpec((1, a, b), lambda e, i: (e, 0, 0))
    return pl.pallas_call(
        _ffn_body,
        grid=(N_EXPERTS, nt),
        in_specs=[smem(lambda e, i: (e * nt + i, 0, 0)),
                  smem(lambda e, i: (jnp.minimum(e * nt + i + 1, ntiles - 1), 0, 0)),
                  pl.BlockSpec(memory_space=pl.ANY),
                  pl.BlockSpec((1, D_MODEL), lambda e, i: (0, 0)),
                  pl.BlockSpec((1, tc, 1), lambda e, i: (e, i, 0)),
                  wspec(D_MODEL, D_EXPERT), wspec(D_MODEL, D_EXPERT), wspec(D_EXPERT, D_MODEL)],
        out_specs=pl.BlockSpec((tc, D_MODEL), lambda e, i: (e * nt + i, 0)),
        out_shape=jax.ShapeDtypeStruct((N_EXPERTS * cap, D_MODEL), BF16),
        scratch_shapes=[pltpu.VMEM((tc, D_MODEL), F32), pltpu.VMEM((tc, D_MODEL), BF16),
                        pltpu.SemaphoreType.DMA((1,))],
        compiler_params=_cparams(("arbitrary", "arbitrary")),
        name="expert_ffn",
    )(idx3, idx3, x, g, gate3, wg, wu, wd)


COMB_TB = 512
COMB_CW = 128
COMB_ALIGN = 16


def _combine_body(boff_ref, x_ref, pos_ref, y_hbm, gfin_ref, o_ref, win, xwin, sem, xsem,
                  *, cap, final_norm):
    b = pl.program_id(0)
    total = N_EXPERTS * cap
    tb = x_ref.shape[0]
    cw = COMB_CW

    slot = b % 2

    def wstart(e, blk):
        s = jnp.bitwise_and(e * cap + boff_ref[e, blk], -COMB_ALIGN)
        return pl.multiple_of(jnp.minimum(s, total - cw), COMB_ALIGN)

    def window_copy(e, start, dst_slot):
        return pltpu.make_async_copy(y_hbm.at[pl.ds(start, cw)],
                                     win.at[dst_slot, pl.ds(e * cw, cw)], sem.at[dst_slot, e])

    @pl.when(b == 0)
    def _():
        for e in range(N_EXPERTS):
            window_copy(e, wstart(e, 0), 0).start()

    @pl.when(b + 1 < pl.num_programs(0))
    def _():
        for e in range(N_EXPERTS):
            window_copy(e, wstart(e, b + 1), 1 - slot).start()

    starts = [wstart(e, b) for e in range(N_EXPERTS)]
    wrow = lax.broadcasted_iota(jnp.int32, (cw, tb), 0).astype(F32)
    pos = pos_ref[...]
    onehot_t = jnp.concatenate(
        [(pos[e:e + 1, :] - starts[e].astype(F32) == wrow).astype(BF16) for e in range(N_EXPERTS)],
        axis=0)
    for e in range(N_EXPERTS):
        window_copy(e, starts[e], slot).wait()
    o_ref[...] = x_ref[...] + _dot_tn(onehot_t, win[slot])

    for e in range(N_EXPERTS):
        end = e * cap + boff_ref[e, b + 1]
        extra = lax.shift_right_logical(jnp.maximum(end - starts[e] - 1, 0), int(math.log2(cw)))

        def more(c, carry, e=e):
            lo = starts[e] + c * cw
            s = pl.multiple_of(jnp.minimum(lo, total - cw), COMB_ALIGN)
            cp = pltpu.make_async_copy(y_hbm.at[pl.ds(s, cw)], xwin, xsem.at[0])
            cp.start()
            cp.wait()
            pe = pos[e:e + 1, :]
            oh = ((pe - s.astype(F32) == wrow) & (pe >= lo.astype(F32))).astype(BF16)
            o_ref[...] += _dot_tn(oh, xwin[...])
            return carry

        lax.fori_loop(1, extra + 1, more, 0)

    if final_norm:
        o_ref[...] = _rms(o_ref[...], gfin_ref[...])


def _combine(x, pos_t, boff, y, gfin, cap, final_norm):
    n = x.shape[0]
    tb = COMB_TB
    grid_spec = pltpu.PrefetchScalarGridSpec(
        num_scalar_prefetch=1,
        grid=(n // tb,),
        in_specs=[pl.BlockSpec((tb, D_MODEL), lambda b, s: (b, 0)),
                  pl.BlockSpec((N_EXPERTS, tb), lambda b, s: (0, b)),
                  pl.BlockSpec(memory_space=pl.ANY),
                  pl.BlockSpec((1, D_MODEL), lambda b, s: (0, 0))],
        out_specs=pl.BlockSpec((tb, D_MODEL), lambda b, s: (b, 0)),
        scratch_shapes=[pltpu.VMEM((2, N_EXPERTS * COMB_CW, D_MODEL), BF16),
                        pltpu.VMEM((COMB_CW, D_MODEL), BF16),
                        pltpu.SemaphoreType.DMA((2, N_EXPERTS)),
                        pltpu.SemaphoreType.DMA((1,))],
    )
    return pl.pallas_call(
        functools.partial(_combine_body, cap=cap, final_norm=final_norm),
        grid_spec=grid_spec,
        out_shape=jax.ShapeDtypeStruct((n, D_MODEL), F32),
        compiler_params=_cparams(("arbitrary",)),
        name="ec_combine",
    )(boff, x, pos_t, y, gfin)


def _moe(xn, aff_t, ffn_norm, wg, wu, wd, gfin, final_norm):
    n = xn.shape[0]
    cap = EC_CAPACITY_FACTOR * n // N_EXPERTS
    idx, gate, pos, off = _route(aff_t, cap)
    y = _expert_ffn(xn, ffn_norm, idx, gate, wg, wu, wd, cap)
    pos_t = pos.reshape(N_EXPERTS, n)
    rows_per_blk = COMB_TB // LANES
    boff = off[:, ::rows_per_blk, 0].astype(jnp.int32)
    boff = jnp.concatenate([boff, jnp.full((N_EXPERTS, 1), cap, jnp.int32)], axis=1)
    return _combine(xn, pos_t, boff, y, gfin, cap, final_norm)


def _expand_kv_columns(w_qkv):
    qd = SW_HEADS * HEAD_DIM
    kd = SW_KV_HEADS * HEAD_DIM
    head_of = jnp.arange(SW_HEADS) // (SW_HEADS // SW_KV_HEADS)
    cols = (head_of[:, None] * HEAD_DIM + jnp.arange(HEAD_DIM)[None, :]).reshape(-1)
    return jnp.concatenate([w_qkv[:, :qd], w_qkv[:, qd:qd + kd][:, cols],
                            w_qkv[:, qd + kd:][:, cols]], axis=1)


def _run_group(x3, layers, final_norm_g):
    bsz, seq_len, _ = x3.shape
    n = bsz * seq_len
    x = x3.reshape(n, D_MODEL)
    cos_t, sin_t = _rope_tables(seq_len)
    gfin = final_norm_g.reshape(1, D_MODEL).astype(F32)
    for i, lp in enumerate(layers):
        kind = MIXER_PATTERN[i % len(MIXER_PATTERN)]
        q, k, v = _qkv_proj(x, lp["mix_norm"], lp["w_qkv"], cos_t, sin_t, seq_len, kind != "nat")
        if kind == "nat":
            o = _nat_attention(q, k, v, lp["nat_bias"], bsz, seq_len)
        elif kind == "swa":
            o = _swa_attention(q, k, v, lp["sink"], bsz, seq_len)
        else:
            lambda_init = 0.8 - 0.6 * math.exp(-0.3 * i)
            o = _diff_attention(q, k, v, lp["lam4"], lp["subln"], bsz, seq_len, lambda_init)
        xn, aff_t = _oproj_router(o, x, lp["w_o"], lp["ffn_norm"], lp["w_router_t"])
        x = _moe(xn, aff_t, lp["ffn_norm"], lp["w_gate"], lp["w_up"], lp["w_down"],
                 gfin, i == len(layers) - 1)
    return x.reshape(bsz, seq_len, D_MODEL)


def _prepare_layer(i, lp):
    kind = MIXER_PATTERN[i % len(MIXER_PATTERN)]
    out = dict(
        mix_norm=lp["mix_norm"].reshape(1, D_MODEL).astype(F32),
        ffn_norm=lp["ffn_norm"].reshape(1, D_MODEL).astype(F32),
        w_o=lp["w_o"].astype(BF16),
        w_router_t=lp["w_router"].T.astype(BF16),
        w_gate=lp["w_gate"].astype(BF16), w_up=lp["w_up"].astype(BF16),
        w_down=lp["w_down"].astype(BF16),
    )
    if kind == "swa":
        out["w_qkv"] = _expand_kv_columns(lp["w_qkv"]).astype(BF16)
        out["sink"] = lp["sink"]
    else:
        out["w_qkv"] = lp["w_qkv"].astype(BF16)
    if kind == "nat":
        out["nat_bias"] = _nat_bias_table(lp["rpb"])
    if kind == "diff":
        out["lam4"] = jnp.stack([lp["lambda_q1"], lp["lambda_k1"], lp["lambda_q2"],
                                 lp["lambda_k2"]]).astype(F32)
        out["subln"] = lp["subln"].reshape(1, LANES).astype(F32)
    return out


def kernel(x_prompt, x_sample, l0_mix_norm, l0_w_qkv, l0_rpb, l0_w_o, l0_ffn_norm, l0_w_router, l0_w_gate, l0_w_up, l0_w_down, l1_mix_norm, l1_w_qkv, l1_sink, l1_w_o, l1_ffn_norm, l1_w_router, l1_w_gate, l1_w_up, l1_w_down, l2_mix_norm, l2_w_qkv, l2_lambda_q1, l2_lambda_k1, l2_lambda_q2, l2_lambda_k2, l2_subln, l2_w_o, l2_ffn_norm, l2_w_router, l2_w_gate, l2_w_up, l2_w_down, l3_mix_norm, l3_w_qkv, l3_rpb, l3_w_o, l3_ffn_norm, l3_w_router, l3_w_gate, l3_w_up, l3_w_down, final_norm):
    layers = [
        dict(mix_norm=l0_mix_norm, w_qkv=l0_w_qkv, rpb=l0_rpb, w_o=l0_w_o, ffn_norm=l0_ffn_norm,
             w_router=l0_w_router, w_gate=l0_w_gate, w_up=l0_w_up, w_down=l0_w_down),
        dict(mix_norm=l1_mix_norm, w_qkv=l1_w_qkv, sink=l1_sink, w_o=l1_w_o, ffn_norm=l1_ffn_norm,
             w_router=l1_w_router, w_gate=l1_w_gate, w_up=l1_w_up, w_down=l1_w_down),
        dict(mix_norm=l2_mix_norm, w_qkv=l2_w_qkv, lambda_q1=l2_lambda_q1, lambda_k1=l2_lambda_k1,
             lambda_q2=l2_lambda_q2, lambda_k2=l2_lambda_k2, subln=l2_subln, w_o=l2_w_o,
             ffn_norm=l2_ffn_norm, w_router=l2_w_router, w_gate=l2_w_gate, w_up=l2_w_up,
             w_down=l2_w_down),
        dict(mix_norm=l3_mix_norm, w_qkv=l3_w_qkv, rpb=l3_rpb, w_o=l3_w_o, ffn_norm=l3_ffn_norm,
             w_router=l3_w_router, w_gate=l3_w_gate, w_up=l3_w_up, w_down=l3_w_down),
    ]
    layers = [_prepare_layer(i, lp) for i, lp in enumerate(layers)]
    y_prompt = _run_group(x_prompt, layers, final_norm)
    y_sample = _run_group(x_sample, layers, final_norm)
    return (y_prompt, y_sample)
```

```python
import functools
import math

import jax
import jax.numpy as jnp
import numpy as np
from jax import lax
from jax.experimental import pallas as pl
from jax.experimental.pallas import tpu as pltpu

D_MODEL = 1024
DEPTH = 4
GRID_W = 64
NA_HEADS = 16
NA_WIN_H = 8
NA_WIN_W = 16
SW_HEADS = 16
SW_KV_HEADS = 4
SW_BLOCK = 128
DIFF_HEADS = 8
HEAD_DIM = 64
N_EXPERTS = 16
EC_CAPACITY_FACTOR = 2
D_EXPERT = 2048
ROPE_THETA = 10000.0
NORM_EPS = 1e-6
MIXER_PATTERN = ("nat", "swa", "diff")

LANES = 128
VMEM_LIMIT = 56 * 1024 * 1024

BF16 = jnp.bfloat16
F32 = jnp.float32
NEG_INF = float("-inf")


def _cparams(sem):
    return pltpu.CompilerParams(dimension_semantics=sem, vmem_limit_bytes=VMEM_LIMIT)


def _dot(a, b):
    return jnp.dot(a, b, preferred_element_type=F32)


def _dot_nt(a, b):
    return lax.dot_general(a, b, (((1,), (1,)), ((), ())), preferred_element_type=F32)


def _dot_tn(a, b):
    return lax.dot_general(a, b, (((0,), (0,)), ((), ())), preferred_element_type=F32)


def _rms(x, g):
    return x * lax.rsqrt(jnp.mean(x * x, axis=-1, keepdims=True) + NORM_EPS) * g


PROJ_TM = 512


def _qkv_body(x_ref, g_ref, w_ref, cos_ref, sin_ref, q_ref, k_ref, v_ref, *, use_rope):
    h = _rms(x_ref[...], g_ref[...])
    y = _dot(h.astype(BF16), w_ref[...])
    tm = y.shape[0]
    lane = lax.broadcasted_iota(jnp.int32, (tm, LANES), 1)
    first_half = (lane % HEAD_DIM) < (HEAD_DIM // 2)
    if use_rope:
        cos = cos_ref[...]
        sin = sin_ref[...]
    outs = (q_ref, k_ref, v_ref)
    for part in range(3):
        for c in range(D_MODEL // LANES):
            col = part * D_MODEL + c * LANES
            blk = y[:, col:col + LANES]
            if use_rope and part < 2:
                rot = jnp.where(first_half,
                                pltpu.roll(blk, LANES - HEAD_DIM // 2, 1),
                                pltpu.roll(blk, HEAD_DIM // 2, 1))
                blk = blk * cos + rot * sin
            if part == 0:
                blk = blk * (HEAD_DIM ** -0.5)
            outs[part][:, c * LANES:(c + 1) * LANES] = blk.astype(BF16)


def _qkv_proj(x, g, w, cos, sin, seq_len, use_rope):
    n = x.shape[0]
    tm = PROJ_TM
    tblocks = seq_len // tm
    out = jax.ShapeDtypeStruct((n, D_MODEL), BF16)
    row = lambda i: (i, 0)
    return pl.pallas_call(
        functools.partial(_qkv_body, use_rope=use_rope),
        grid=(n // tm,),
        in_specs=[
            pl.BlockSpec((tm, D_MODEL), row),
            pl.BlockSpec((1, D_MODEL), lambda i: (0, 0)),
            pl.BlockSpec((D_MODEL, 3 * D_MODEL), lambda i: (0, 0)),
            pl.BlockSpec((tm, LANES), lambda i: (i % tblocks, 0)),
            pl.BlockSpec((tm, LANES), lambda i: (i % tblocks, 0)),
        ],
        out_specs=[pl.BlockSpec((tm, D_MODEL), row)] * 3,
        out_shape=[out, out, out],
        compiler_params=_cparams(("parallel",)),
        name="qkv_proj",
    )(x, g, w, cos, sin)


def _rope_tables(seq_len):
    inv = ROPE_THETA ** (-jnp.arange(0, HEAD_DIM, 2, dtype=F32) / HEAD_DIM)
    ang = jnp.arange(seq_len, dtype=F32)[:, None] * inv[None, :]
    cos = jnp.cos(ang)
    sin = jnp.sin(ang)
    cos_t = jnp.tile(jnp.concatenate([cos, cos], axis=1), (1, LANES // HEAD_DIM))
    sin_t = jnp.tile(jnp.concatenate([-sin, sin], axis=1), (1, LANES // HEAD_DIM))
    return cos_t, sin_t


def _pair_split(qp):
    lane = lax.broadcasted_iota(jnp.int32, qp.shape, 1)
    zero = jnp.zeros_like(qp)
    return jnp.concatenate([jnp.where(lane < HEAD_DIM, qp, zero),
                            jnp.where(lane >= HEAD_DIM, qp, zero)], axis=0)


def _pair_merge(o, m):
    lane = lax.broadcasted_iota(jnp.int32, (m, LANES), 1)
    return jnp.where(lane < HEAD_DIM, o[:m], o[m:])


NAT_ROWS_PER_STEP = 4
NAT_TQ = NAT_ROWS_PER_STEP * GRID_W
NAT_WIN_BLOCKS = 3
NAT_PAIRS_PER_STEP = 8


def _nat_body(q_ref, k0_ref, k1_ref, k2_ref, v0_ref, v1_ref, v2_ref, b_ref, o_ref):
    krefs = (k0_ref, k1_ref, k2_ref)
    vrefs = (v0_ref, v1_ref, v2_ref)
    for p in range(NAT_PAIRS_PER_STEP):
        cs = slice(p * LANES, (p + 1) * LANES)
        qs = _pair_split(q_ref[:, cs])
        s = [_dot_nt(qs, krefs[j][:, cs]) + b_ref[0, p, :, j * NAT_TQ:(j + 1) * NAT_TQ]
             for j in range(NAT_WIN_BLOCKS)]
        m = jnp.maximum(jnp.maximum(jnp.max(s[0], axis=-1, keepdims=True),
                                    jnp.max(s[1], axis=-1, keepdims=True)),
                        jnp.max(s[2], axis=-1, keepdims=True))
        e = [jnp.exp(sj - m) for sj in s]
        l = (jnp.sum(e[0], axis=-1, keepdims=True) + jnp.sum(e[1], axis=-1, keepdims=True)
             + jnp.sum(e[2], axis=-1, keepdims=True))
        r = 1.0 / l
        o = (_dot((e[0] * r).astype(BF16), vrefs[0][:, cs])
             + _dot((e[1] * r).astype(BF16), vrefs[1][:, cs])
             + _dot((e[2] * r).astype(BF16), vrefs[2][:, cs]))
        o_ref[:, cs] = _pair_merge(o, NAT_TQ).astype(BF16)


def _nat_bias_table(rpb):
    cols = jnp.arange(GRID_W)
    col_start = jnp.clip(cols - NA_WIN_W // 2, 0, GRID_W - NA_WIN_W)
    col_in = (cols[None, :] >= col_start[:, None]) & (cols[None, :] < col_start[:, None] + NA_WIN_W)
    dc_idx = jnp.clip(cols[None, :] - cols[:, None] + NA_WIN_W - 1, 0, 2 * NA_WIN_W - 2)
    tz = jnp.where(col_in[None, None], rpb.astype(F32)[:, :, dc_idx], NEG_INF)
    n_dr = 2 * NA_WIN_H - 1
    tz = jnp.concatenate([tz, jnp.full((NA_HEADS, 1, GRID_W, GRID_W), NEG_INF, F32)], axis=1)
    win_rows = NAT_WIN_BLOCKS * NAT_ROWS_PER_STEP
    half = NA_WIN_H // 2
    d_idx = np.full((3, NAT_ROWS_PER_STEP, win_rows), n_dr, np.int32)
    for case in range(3):
        for a in range(NAT_ROWS_PER_STEP):
            j0, d0 = ((0, NA_WIN_H - 1 - a), (a, NA_WIN_H - 1 - half), (half, NA_WIN_H - 1 - half - a))[case]
            for jj in range(NA_WIN_H):
                d_idx[case, a, j0 + jj] = d0 + jj
    t = tz[:, d_idx]
    t = jnp.transpose(t, (1, 0, 2, 4, 3, 5))
    return t.reshape(3, NA_HEADS // 2, 2 * NAT_TQ, win_rows * GRID_W)


def _nat_attention(q, k, v, bias, bsz, seq_len):
    n = q.shape[0]
    rows = seq_len // GRID_W
    nb = rows // NAT_ROWS_PER_STEP
    assert rows % NAT_ROWS_PER_STEP == 0 and nb >= NAT_WIN_BLOCKS
    pp = NAT_PAIRS_PER_STEP
    width = pp * LANES

    def win(j):
        return lambda hp, b, i: (b * nb + jnp.clip(i - 1, 0, nb - NAT_WIN_BLOCKS) + j, hp)

    def bias_map(hp, b, i):
        case = jnp.where(i == 0, 0, jnp.where(i == nb - 1, 2, 1))
        return (case, hp, 0, 0)

    qmap = lambda hp, b, i: (b * nb + i, hp)
    blk = lambda m: pl.BlockSpec((NAT_TQ, width), m)
    return pl.pallas_call(
        _nat_body,
        grid=(NA_HEADS // 2 // pp, bsz, nb),
        in_specs=[blk(qmap), blk(win(0)), blk(win(1)), blk(win(2)),
                  blk(win(0)), blk(win(1)), blk(win(2)),
                  pl.BlockSpec((1, pp, 2 * NAT_TQ, NAT_WIN_BLOCKS * NAT_TQ), bias_map)],
        out_specs=blk(qmap),
        out_shape=jax.ShapeDtypeStruct((n, D_MODEL), BF16),
        compiler_params=_cparams(("parallel", "parallel", "parallel")),
        name="nat_attention",
    )(q, k, k, k, v, v, v, bias)


def _swa_body(sink_ref, q_ref, k0_ref, k1_ref, k2_ref, v0_ref, v1_ref, v2_ref, o_ref, *, nblk):
    i = pl.program_id(1)
    tb = SW_BLOCK
    group = SW_HEADS // SW_KV_HEADS
    qq = lax.broadcasted_iota(jnp.int32, (group * tb, 3 * tb), 0) % tb
    kk = lax.broadcasted_iota(jnp.int32, (group * tb, 3 * tb), 1) - tb
    valid = ((jnp.abs(kk - qq) <= tb) & ((kk >= 0) | (i > 0)) & ((kk < tb) | (i < nblk - 1)))
    row = lax.broadcasted_iota(jnp.int32, (group * tb, 1), 0)
    for u in range(SW_KV_HEADS):
        pairs = [slice((2 * u + t) * LANES, (2 * u + t + 1) * LANES) for t in range(group // 2)]
        qs = jnp.concatenate([_pair_split(q_ref[:, cs]) for cs in pairs], axis=0)
        kc = jnp.concatenate([k0_ref[:, pairs[0]], k1_ref[:, pairs[0]], k2_ref[:, pairs[0]]], axis=0)
        vc = jnp.concatenate([v0_ref[:, pairs[0]], v1_ref[:, pairs[0]], v2_ref[:, pairs[0]]], axis=0)
        s = jnp.where(valid, _dot_nt(qs, kc), NEG_INF)
        sink = sink_ref[group * u + group - 1]
        for h in reversed(range(group - 1)):
            sink = jnp.where(row < (h + 1) * tb, sink_ref[group * u + h], sink)
        m = jnp.maximum(jnp.max(s, axis=-1, keepdims=True), sink)
        e = jnp.exp(s - m)
        l = jnp.sum(e, axis=-1, keepdims=True) + jnp.exp(sink - m)
        o = _dot((e / l).astype(BF16), vc)
        for t, cs in enumerate(pairs):
            o_ref[:, cs] = _pair_merge(o[2 * t * tb:(2 * t + 2) * tb], tb).astype(BF16)


def _swa_attention(q, k, v, sink, bsz, seq_len):
    n = q.shape[0]
    nblk = seq_len // SW_BLOCK
    cur = lambda b, i: (b * nblk + i, 0)
    prev = lambda b, i: (b * nblk + jnp.maximum(i - 1, 0), 0)
    nxt = lambda b, i: (b * nblk + jnp.minimum(i + 1, nblk - 1), 0)
    blk = lambda m: pl.BlockSpec((SW_BLOCK, D_MODEL), m)
    return pl.pallas_call(
        functools.partial(_swa_body, nblk=nblk),
        grid=(bsz, nblk),
        in_specs=[pl.BlockSpec(memory_space=pltpu.SMEM),
                  blk(cur), blk(prev), blk(cur), blk(nxt), blk(prev), blk(cur), blk(nxt)],
        out_specs=blk(cur),
        out_shape=jax.ShapeDtypeStruct((n, D_MODEL), BF16),
        compiler_params=_cparams(("parallel", "parallel")),
        name="swa_attention",
    )(sink.astype(F32), q, k, k, k, v, v, v)


DIFF_TQ = 2048
DIFF_TK = 2048
DIFF_ROW_CHUNKS = 4


def _diff_body(q_ref, k_ref, v_ref, lam_ref, subln_ref, o_ref, qs_sc, m_sc, l_sc, acc_sc,
               *, lambda_init):
    kv = pl.program_id(3)
    tq = q_ref.shape[0]
    tk = k_ref.shape[0]

    @pl.when(kv == 0)
    def _():
        qs_sc[...] = _pair_split(q_ref[...])
        m_sc[...] = jnp.full(m_sc.shape, NEG_INF, F32)
        l_sc[...] = jnp.zeros(l_sc.shape, F32)
        acc_sc[...] = jnp.zeros(acc_sc.shape, F32)

    k = k_ref[...]
    v = v_ref[...]
    rows = 2 * tq // DIFF_ROW_CHUNKS
    for c in range(DIFF_ROW_CHUNKS):
        rs = slice(c * rows, (c + 1) * rows)
        s = _dot_nt(qs_sc[rs, :], k)
        m_prev = m_sc[rs, :]
        m_new = jnp.maximum(m_prev, jnp.max(s, axis=-1, keepdims=True))
        alpha = jnp.exp(m_prev - m_new)
        p = jnp.exp(s - jnp.concatenate([m_new] * (tk // LANES), axis=1))
        l_sc[rs, :] = alpha * l_sc[rs, :] + jnp.sum(p, axis=-1, keepdims=True)
        acc_sc[rs, :] = alpha * acc_sc[rs, :] + _dot(p.astype(BF16), v)
        m_sc[rs, :] = m_new

    @pl.when(kv == pl.num_programs(3) - 1)
    def _():
        o12 = acc_sc[...] / l_sc[...]
        lam4 = lam_ref[...]
        lam = (jnp.exp(jnp.sum(lam4[0:1] * lam4[1:2], axis=-1, keepdims=True))
               - jnp.exp(jnp.sum(lam4[2:3] * lam4[3:4], axis=-1, keepdims=True)) + lambda_init)
        o = o12[:tq] - lam * o12[tq:]
        o = _rms(o, subln_ref[...]) * (1.0 - lambda_init)
        o_ref[...] = o.astype(BF16)


def _diff_attention(q, k, v, lam4, subln, bsz, seq_len, lambda_init):
    n = q.shape[0]
    tq = min(DIFF_TQ, seq_len)
    tk = min(DIFF_TK, seq_len)
    nq = seq_len // tq
    nk = seq_len // tk
    return pl.pallas_call(
        functools.partial(_diff_body, lambda_init=lambda_init),
        grid=(bsz, DIFF_HEADS, nq, nk),
        in_specs=[
            pl.BlockSpec((tq, LANES), lambda b, h, qi, ki: (b * nq + qi, h)),
            pl.BlockSpec((tk, LANES), lambda b, h, qi, ki: (b * nk + ki, h)),
            pl.BlockSpec((tk, LANES), lambda b, h, qi, ki: (b * nk + ki, h)),
            pl.BlockSpec((4, HEAD_DIM), lambda b, h, qi, ki: (0, 0)),
            pl.BlockSpec((1, LANES), lambda b, h, qi, ki: (0, 0)),
        ],
        out_specs=pl.BlockSpec((tq, LANES), lambda b, h, qi, ki: (b * nq + qi, h)),
        out_shape=jax.ShapeDtypeStruct((n, D_MODEL), BF16),
        scratch_shapes=[pltpu.VMEM((2 * tq, LANES), BF16), pltpu.VMEM((2 * tq, LANES), F32),
                        pltpu.VMEM((2 * tq, LANES), F32), pltpu.VMEM((2 * tq, LANES), F32)],
        compiler_params=_cparams(("parallel", "parallel", "parallel", "arbitrary")),
        name="diff_attention",
    )(q, k, v, lam4, subln)


def _oproj_body(o_ref, x_ref, wo_ref, g_ref, wr_ref, xn_ref, aff_ref):
    xn = x_ref[...] + _dot(o_ref[...], wo_ref[...])
    xn_ref[...] = xn
    h = _rms(xn, g_ref[...]).astype(BF16)
    logits = _dot_nt(wr_ref[...], h)
    m = jnp.max(logits, axis=0, keepdims=True)
    e = jnp.exp(logits - m)
    aff_ref[...] = e / jnp.sum(e, axis=0, keepdims=True)


OPROJ_TM = 512


def _oproj_router(o, x, wo, g, wr_t):
    n = x.shape[0]
    tm = OPROJ_TM
    row = lambda i: (i, 0)
    const = lambda i: (0, 0)
    return pl.pallas_call(
        _oproj_body,
        grid=(n // tm,),
        in_specs=[pl.BlockSpec((tm, D_MODEL), row), pl.BlockSpec((tm, D_MODEL), row),
                  pl.BlockSpec((D_MODEL, D_MODEL), const), pl.BlockSpec((1, D_MODEL), const),
                  pl.BlockSpec((N_EXPERTS, D_MODEL), const)],
        out_specs=[pl.BlockSpec((tm, D_MODEL), row), pl.BlockSpec((N_EXPERTS, tm), lambda i: (0, i))],
        out_shape=[jax.ShapeDtypeStruct((n, D_MODEL), F32),
                   jax.ShapeDtypeStruct((N_EXPERTS, n), F32)],
        compiler_params=_cparams(("parallel",)),
        name="oproj_router",
    )(o, x, wo, g, wr_t)


def _route_body(a_ref, idx_ref, gate_ref, pos_ref, off_ref, *, cap):
    e_id = pl.program_id(0)
    a = a_ref[0]
    nr = a.shape[0]
    bits = pltpu.bitcast(a, jnp.int32)

    def count(mask):
        c = jnp.sum(mask.astype(F32), axis=1, keepdims=True)
        return jnp.sum(c, axis=0, keepdims=True)

    def search(step, thr):
        cand = thr | lax.shift_left(jnp.int32(1), 30 - step)
        return jnp.where(count(bits >= cand) >= cap, cand, thr)

    thr = lax.fori_loop(0, 31, search, jnp.zeros((1, 1), jnp.int32))
    gt = bits > thr
    eq = bits == thr
    need = cap - count(gt)

    li = lax.broadcasted_iota(jnp.int32, (LANES, LANES), 0)
    lj = lax.broadcasted_iota(jnp.int32, (LANES, LANES), 1)
    upper_incl = (li <= lj).astype(BF16)
    ri = lax.broadcasted_iota(jnp.int32, (nr, nr), 0)
    rj = lax.broadcasted_iota(jnp.int32, (nr, nr), 1)
    lower_strict = (rj < ri).astype(BF16)

    def prefix(mask):
        w = _dot(mask.astype(BF16), upper_incl)
        tot = jnp.broadcast_to(w[:, LANES - 1:LANES], (nr, LANES))
        off = _dot(lower_strict, tot.astype(BF16))
        return w, tot, off

    w_eq, _, off_eq = prefix(eq)
    sel = gt | (eq & (off_eq + w_eq <= need))
    w, tot, off = prefix(sel)
    base = (e_id * cap).astype(F32)
    pos_ref[0] = jnp.where(sel, base + off + w - 1.0, -1.0)
    off_ref[0] = off

    off_excl = off[:, 0:1]
    off_incl = off_excl + tot[:, 0:1]
    w_t = w.T.astype(BF16)
    a_t = a.T
    a_hi_t = a_t.astype(BF16)
    r1 = a_t - a_hi_t.astype(F32)
    a_mid_t = r1.astype(BF16)
    a_lo_t = (r1 - a_mid_t.astype(F32)).astype(BF16)
    st = idx_ref.shape[2]
    r_col = lax.broadcasted_iota(jnp.int32, (nr, st), 0).astype(F32)
    l_col = lax.broadcasted_iota(jnp.int32, (LANES, st), 0).astype(F32)

    def slot_tile(t, carry):
        j = (t * st + lax.broadcasted_iota(jnp.int32, (1, st), 1)).astype(F32)
        hit = (off_excl <= j) & (j < off_incl)
        hit_b = hit.astype(BF16)
        k = j - jnp.sum(jnp.where(hit, off_excl, 0.0), axis=0, keepdims=True)
        row = jnp.sum(jnp.where(hit, r_col, 0.0), axis=0, keepdims=True)
        wsel = _dot(w_t, hit_b)
        lane = jnp.sum((wsel <= k).astype(F32), axis=0, keepdims=True)
        asel = _dot(a_hi_t, hit_b) + _dot(a_mid_t, hit_b) + _dot(a_lo_t, hit_b)
        gate = jnp.sum(jnp.where(l_col == lane, asel, 0.0), axis=0, keepdims=True)
        idx_ref[0, pl.ds(t, 1), :] = (row * LANES + lane).astype(jnp.int32)
        gate_ref[0, pl.ds(t, 1), :] = gate
        return carry

    lax.fori_loop(0, cap // st, slot_tile, 0)


ROUTE_SLOT_TILE = 512


def _route(aff_t, cap):
    n = aff_t.shape[1]
    nr = n // LANES
    st = min(ROUTE_SLOT_TILE, cap)
    a3 = aff_t.reshape(N_EXPERTS, nr, LANES)
    blk = lambda r, c=LANES: pl.BlockSpec((1, r, c), lambda e: (e, 0, 0))
    return pl.pallas_call(
        functools.partial(_route_body, cap=cap),
        grid=(N_EXPERTS,),
        in_specs=[blk(nr)],
        out_specs=[blk(cap // st, st), blk(cap // st, st), blk(nr), blk(nr)],
        out_shape=[jax.ShapeDtypeStruct((N_EXPERTS, cap // st, st), jnp.int32),
                   jax.ShapeDtypeStruct((N_EXPERTS, cap // st, st), F32),
                   jax.ShapeDtypeStruct((N_EXPERTS, nr, LANES), F32),
                   jax.ShapeDtypeStruct((N_EXPERTS, nr, LANES), F32)],
        compiler_params=_cparams(("parallel",)),
        name="ec_route",
    )(a3)


FFN_TC = 512
FFN_FCHUNK = 512


def _ffn_body(idx_cur, idx_nxt, x_hbm, g_ref, gate_ref, wg_ref, wu_ref, wd_ref, y_ref,
              xbuf, xe_sc, sem):
    tc = xbuf.shape[0]
    step = pl.program_id(0) * pl.num_programs(1) + pl.program_id(1)
    total = pl.num_programs(0) * pl.num_programs(1)

    def row_copy(idx, r):
        return pltpu.make_async_copy(x_hbm.at[pl.ds(idx[0, 0, r], 1)], xbuf.at[pl.ds(r, 1)],
                                     sem.at[0])

    def wait_rows():
        pltpu.make_async_copy(x_hbm.at[pl.ds(0, tc)], xbuf, sem.at[0]).wait()

    @pl.when(step == 0)
    def _():
        def one(r, c):
            row_copy(idx_cur, r).start()
            return c
        lax.fori_loop(0, tc, one, 0)

    wait_rows()
    xe_sc[...] = _rms(xbuf[...], g_ref[...]).astype(BF16)
    acc = jnp.zeros((tc, D_MODEL), F32)
    nchunk = D_EXPERT // FFN_FCHUNK
    rows_per_chunk = tc // nchunk
    for c in range(nchunk):
        for r in range(c * rows_per_chunk, (c + 1) * rows_per_chunk):
            row_copy(idx_nxt, r).start(priority=r % 2)
        fs = slice(c * FFN_FCHUNK, (c + 1) * FFN_FCHUNK)
        xe = xe_sc[...]
        hg = _dot(xe, wg_ref[0, :, fs])
        hu = _dot(xe, wu_ref[0, :, fs])
        hid = (hg * jax.nn.sigmoid(hg)) * hu
        acc = acc + _dot(hid.astype(BF16), wd_ref[0, fs, :])
    y_ref[...] = (acc * gate_ref[0]).astype(BF16)

    @pl.when(step == total - 1)
    def _():
        wait_rows()


def _expert_ffn(x, g, idx, gate, wg, wu, wd, cap):
    tc = min(FFN_TC, cap)
    nt = cap // tc
    ntiles = N_EXPERTS * nt
    idx3 = idx.reshape(ntiles, 1, tc)
    gate3 = gate.reshape(N_EXPERTS, cap, 1)
    smem = lambda m: pl.BlockSpec((1, 1, tc), m, memory_space=pltpu.SMEM)
    wspec = lambda a, b: pl.BlockSpec((1, a, b), lambda e, i: (e, 0, 0))
    return pl.pallas_call(
        _ffn_body,
        grid=(N_EXPERTS, nt),
        in_specs=[smem(lambda e, i: (e * nt + i, 0, 0)),
                  smem(lambda e, i: (jnp.minimum(e * nt + i + 1, ntiles - 1), 0, 0)),
                  pl.BlockSpec(memory_space=pl.ANY),
                  pl.BlockSpec((1, D_MODEL), lambda e, i: (0, 0)),
                  pl.BlockSpec((1, tc, 1), lambda e, i: (e, i, 0)),
                  wspec(D_MODEL, D_EXPERT), wspec(D_MODEL, D_EXPERT), wspec(D_EXPERT, D_MODEL)],
        out_specs=pl.BlockSpec((tc, D_MODEL), lambda e, i: (e * nt + i, 0)),
        out_shape=jax.ShapeDtypeStruct((N_EXPERTS * cap, D_MODEL), BF16),
        scratch_shapes=[pltpu.VMEM((tc, D_MODEL), F32), pltpu.VMEM((tc, D_MODEL), BF16),
                        pltpu.SemaphoreType.DMA((1,))],
        compiler_params=_cparams(("arbitrary", "arbitrary")),
        name="expert_ffn",
    )(idx3, idx3, x, g, gate3, wg, wu, wd)


COMB_TB = 512
COMB_CW = 128
COMB_ALIGN = 16


def _combine_body(boff_ref, x_ref, pos_ref, y_hbm, gfin_ref, o_ref, win, xwin, sem, xsem,
                  *, cap, final_norm):
    b = pl.program_id(0)
    total = N_EXPERTS * cap
    tb = x_ref.shape[0]
    cw = COMB_CW

    slot = b % 2

    def wstart(e, blk):
        s = jnp.bitwise_and(e * cap + boff_ref[e, blk], -COMB_ALIGN)
        return pl.multiple_of(jnp.minimum(s, total - cw), COMB_ALIGN)

    def window_copy(e, start, dst_slot):
        return pltpu.make_async_copy(y_hbm.at[pl.ds(start, cw)],
                                     win.at[dst_slot, pl.ds(e * cw, cw)], sem.at[dst_slot, e])

    @pl.when(b == 0)
    def _():
        for e in range(N_EXPERTS):
            window_copy(e, wstart(e, 0), 0).start()

    @pl.when(b + 1 < pl.num_programs(0))
    def _():
        for e in range(N_EXPERTS):
            window_copy(e, wstart(e, b + 1), 1 - slot).start()

    starts = [wstart(e, b) for e in range(N_EXPERTS)]
    wrow = lax.broadcasted_iota(jnp.int32, (cw, tb), 0).astype(F32)
    pos = pos_ref[...]
    onehot_t = jnp.concatenate(
        [(pos[e:e + 1, :] - starts[e].astype(F32) == wrow).astype(BF16) for e in range(N_EXPERTS)],
        axis=0)
    for e in range(N_EXPERTS):
        window_copy(e, starts[e], slot).wait()
    o_ref[...] = x_ref[...] + _dot_tn(onehot_t, win[slot])

    for e in range(N_EXPERTS):
        end = e * cap + boff_ref[e, b + 1]
        extra = lax.shift_right_logical(jnp.maximum(end - starts[e] - 1, 0), int(math.log2(cw)))

        def more(c, carry, e=e):
            lo = starts[e] + c * cw
            s = pl.multiple_of(jnp.minimum(lo, total - cw), COMB_ALIGN)
            cp = pltpu.make_async_copy(y_hbm.at[pl.ds(s, cw)], xwin, xsem.at[0])
            cp.start()
            cp.wait()
            pe = pos[e:e + 1, :]
            oh = ((pe - s.astype(F32) == wrow) & (pe >= lo.astype(F32))).astype(BF16)
            o_ref[...] += _dot_tn(oh, xwin[...])
            return carry

        lax.fori_loop(1, extra + 1, more, 0)

    if final_norm:
        o_ref[...] = _rms(o_ref[...], gfin_ref[...])


def _combine(x, pos_t, boff, y, gfin, cap, final_norm):
    n = x.shape[0]
    tb = COMB_TB
    grid_spec = pltpu.PrefetchScalarGridSpec(
        num_scalar_prefetch=1,
        grid=(n // tb,),
        in_specs=[pl.BlockSpec((tb, D_MODEL), lambda b, s: (b, 0)),
                  pl.BlockSpec((N_EXPERTS, tb), lambda b, s: (0, b)),
                  pl.BlockSpec(memory_space=pl.ANY),
                  pl.BlockSpec((1, D_MODEL), lambda b, s: (0, 0))],
        out_specs=pl.BlockSpec((tb, D_MODEL), lambda b, s: (b, 0)),
        scratch_shapes=[pltpu.VMEM((2, N_EXPERTS * COMB_CW, D_MODEL), BF16),
                        pltpu.VMEM((COMB_CW, D_MODEL), BF16),
                        pltpu.SemaphoreType.DMA((2, N_EXPERTS)),
                        pltpu.SemaphoreType.DMA((1,))],
    )
    return pl.pallas_call(
        functools.partial(_combine_body, cap=cap, final_norm=final_norm),
        grid_spec=grid_spec,
        out_shape=jax.ShapeDtypeStruct((n, D_MODEL), F32),
        compiler_params=_cparams(("arbitrary",)),
        name="ec_combine",
    )(boff, x, pos_t, y, gfin)


def _moe(xn, aff_t, ffn_norm, wg, wu, wd, gfin, final_norm):
    n = xn.shape[0]
    cap = EC_CAPACITY_FACTOR * n // N_EXPERTS
    idx, gate, pos, off = _route(aff_t, cap)
    y = _expert_ffn(xn, ffn_norm, idx, gate, wg, wu, wd, cap)
    pos_t = pos.reshape(N_EXPERTS, n)
    rows_per_blk = COMB_TB // LANES
    boff = off[:, ::rows_per_blk, 0].astype(jnp.int32)
    boff = jnp.concatenate([boff, jnp.full((N_EXPERTS, 1), cap, jnp.int32)], axis=1)
    return _combine(xn, pos_t, boff, y, gfin, cap, final_norm)


def _expand_kv_columns(w_qkv):
    qd = SW_HEADS * HEAD_DIM
    kd = SW_KV_HEADS * HEAD_DIM
    head_of = jnp.arange(SW_HEADS) // (SW_HEADS // SW_KV_HEADS)
    cols = (head_of[:, None] * HEAD_DIM + jnp.arange(HEAD_DIM)[None, :]).reshape(-1)
    return jnp.concatenate([w_qkv[:, :qd], w_qkv[:, qd:qd + kd][:, cols],
                            w_qkv[:, qd + kd:][:, cols]], axis=1)


def _run_group(x3, layers, final_norm_g):
    bsz, seq_len, _ = x3.shape
    n = bsz * seq_len
    x = x3.reshape(n, D_MODEL)
    cos_t, sin_t = _rope_tables(seq_len)
    gfin = final_norm_g.reshape(1, D_MODEL).astype(F32)
    for i, lp in enumerate(layers):
        kind = MIXER_PATTERN[i % len(MIXER_PATTERN)]
        q, k, v = _qkv_proj(x, lp["mix_norm"], lp["w_qkv"], cos_t, sin_t, seq_len, kind != "nat")
        if kind == "nat":
            o = _nat_attention(q, k, v, lp["nat_bias"], bsz, seq_len)
        elif kind == "swa":
            o = _swa_attention(q, k, v, lp["sink"], bsz, seq_len)
        else:
            lambda_init = 0.8 - 0.6 * math.exp(-0.3 * i)
            o = _diff_attention(q, k, v, lp["lam4"], lp["subln"], bsz, seq_len, lambda_init)
        xn, aff_t = _oproj_router(o, x, lp["w_o"], lp["ffn_norm"], lp["w_router_t"])
        x = _moe(xn, aff_t, lp["ffn_norm"], lp["w_gate"], lp["w_up"], lp["w_down"],
                 gfin, i == len(layers) - 1)
    return x.reshape(bsz, seq_len, D_MODEL)


def _prepare_layer(i, lp):
    kind = MIXER_PATTERN[i % len(MIXER_PATTERN)]
    out = dict(
        mix_norm=lp["mix_norm"].reshape(1, D_MODEL).astype(F32),
        ffn_norm=lp["ffn_norm"].reshape(1, D_MODEL).astype(F32),
        w_o=lp["w_o"].astype(BF16),
        w_router_t=lp["w_router"].T.astype(BF16),
        w_gate=lp["w_gate"].astype(BF16), w_up=lp["w_up"].astype(BF16),
        w_down=lp["w_down"].astype(BF16),
    )
    if kind == "swa":
        out["w_qkv"] = _expand_kv_columns(lp["w_qkv"]).astype(BF16)
        out["sink"] = lp["sink"]
    else:
        out["w_qkv"] = lp["w_qkv"].astype(BF16)
    if kind == "nat":
        out["nat_bias"] = _nat_bias_table(lp["rpb"])
    if kind == "diff":
        out["lam4"] = jnp.stack([lp["lambda_q1"], lp["lambda_k1"], lp["lambda_q2"],
                                 lp["lambda_k2"]]).astype(F32)
        out["subln"] = lp["subln"].reshape(1, LANES).astype(F32)
    return out


def kernel(x_prompt, x_sample, l0_mix_norm, l0_w_qkv, l0_rpb, l0_w_o, l0_ffn_norm, l0_w_router, l0_w_gate, l0_w_up, l0_w_down, l1_mix_norm, l1_w_qkv, l1_sink, l1_w_o, l1_ffn_norm, l1_w_router, l1_w_gate, l1_w_up, l1_w_down, l2_mix_norm, l2_w_qkv, l2_lambda_q1, l2_lambda_k1, l2_lambda_q2, l2_lambda_k2, l2_subln, l2_w_o, l2_ffn_norm, l2_w_router, l2_w_gate, l2_w_up, l2_w_down, l3_mix_norm, l3_w_qkv, l3_rpb, l3_w_o, l3_ffn_norm, l3_w_router, l3_w_gate, l3_w_up, l3_w_down, final_norm):
    layers = [
        dict(mix_norm=l0_mix_norm, w_qkv=l0_w_qkv, rpb=l0_rpb, w_o=l0_w_o, ffn_norm=l0_ffn_norm,
             w_router=l0_w_router, w_gate=l0_w_gate, w_up=l0_w_up, w_down=l0_w_down),
        dict(mix_norm=l1_mix_norm, w_qkv=l1_w_qkv, sink=l1_sink, w_o=l1_w_o, ffn_norm=l1_ffn_norm,
             w_router=l1_w_router, w_gate=l1_w_gate, w_up=l1_w_up, w_down=l1_w_down),
        dict(mix_norm=l2_mix_norm, w_qkv=l2_w_qkv, lambda_q1=l2_lambda_q1, lambda_k1=l2_lambda_k1,
             lambda_q2=l2_lambda_q2, lambda_k2=l2_lambda_k2, subln=l2_subln, w_o=l2_w_o,
             ffn_norm=l2_ffn_norm, w_router=l2_w_router, w_gate=l2_w_gate, w_up=l2_w_up,
             w_down=l2_w_down),
        dict(mix_norm=l3_mix_norm, w_qkv=l3_w_qkv, rpb=l3_rpb, w_o=l3_w_o, ffn_norm=l3_ffn_norm,
             w_router=l3_w_router, w_gate=l3_w_gate, w_up=l3_w_up, w_down=l3_w_down),
    ]
    layers = [_prepare_layer(i, lp) for i, lp in enumerate(layers)]
    y_prompt = _run_group(x_prompt, layers, final_norm)
    y_sample = _run_group(x_sample, layers, final_norm)
    return (y_prompt, y_sample)
```

```python
import functools
import math

import jax
import jax.numpy as jnp
import numpy as np
from jax import lax
from jax.experimental import pallas as pl
from jax.experimental.pallas import tpu as pltpu

D_MODEL = 1024
DEPTH = 4
GRID_W = 64
NA_HEADS = 16
NA_WIN_H = 8
NA_WIN_W = 16
SW_HEADS = 16
SW_KV_HEADS = 4
SW_BLOCK = 128
DIFF_HEADS = 8
HEAD_DIM = 64
N_EXPERTS = 16
EC_CAPACITY_FACTOR = 2
D_EXPERT = 2048
ROPE_THETA = 10000.0
NORM_EPS = 1e-6
MIXER_PATTERN = ("nat", "swa", "diff")

LANES = 128
VMEM_LIMIT = 56 * 1024 * 1024

BF16 = jnp.bfloat16
F32 = jnp.float32
NEG_INF = float("-inf")


def _cparams(sem):
    return pltpu.CompilerParams(dimension_semantics=sem, vmem_limit_bytes=VMEM_LIMIT)


def _dot(a, b):
    return jnp.dot(a, b, preferred_element_type=F32)


def _dot_nt(a, b):
    return lax.dot_general(a, b, (((1,), (1,)), ((), ())), preferred_element_type=F32)


def _dot_tn(a, b):
    return lax.dot_general(a, b, (((0,), (0,)), ((), ())), preferred_element_type=F32)


def _rms(x, g):
    return x * lax.rsqrt(jnp.mean(x * x, axis=-1, keepdims=True) + NORM_EPS) * g


PROJ_TM = 512


def _qkv_body(x_ref, g_ref, w_ref, cos_ref, sin_ref, q_ref, k_ref, v_ref, *, use_rope):
    h = _rms(x_ref[...], g_ref[...])
    y = _dot(h.astype(BF16), w_ref[...])
    tm = y.shape[0]
    lane = lax.broadcasted_iota(jnp.int32, (tm, LANES), 1)
    first_half = (lane % HEAD_DIM) < (HEAD_DIM // 2)
    if use_rope:
        cos = cos_ref[...]
        sin = sin_ref[...]
    outs = (q_ref, k_ref, v_ref)
    for part in range(3):
        for c in range(D_MODEL // LANES):
            col = part * D_MODEL + c * LANES
            blk = y[:, col:col + LANES]
            if use_rope and part < 2:
                rot = jnp.where(first_half,
                                pltpu.roll(blk, LANES - HEAD_DIM // 2, 1),
                                pltpu.roll(blk, HEAD_DIM // 2, 1))
                blk = blk * cos + rot * sin
            if part == 0:
                blk = blk * (HEAD_DIM ** -0.5)
            outs[part][:, c * LANES:(c + 1) * LANES] = blk.astype(BF16)


def _qkv_proj(x, g, w, cos, sin, seq_len, use_rope):
    n = x.shape[0]
    tm = PROJ_TM
    tblocks = seq_len // tm
    out = jax.ShapeDtypeStruct((n, D_MODEL), BF16)
    row = lambda i: (i, 0)
    return pl.pallas_call(
        functools.partial(_qkv_body, use_rope=use_rope),
        grid=(n // tm,),
        in_specs=[
            pl.BlockSpec((tm, D_MODEL), row),
            pl.BlockSpec((1, D_MODEL), lambda i: (0, 0)),
            pl.BlockSpec((D_MODEL, 3 * D_MODEL), lambda i: (0, 0)),
            pl.BlockSpec((tm, LANES), lambda i: (i % tblocks, 0)),
            pl.BlockSpec((tm, LANES), lambda i: (i % tblocks, 0)),
        ],
        out_specs=[pl.BlockSpec((tm, D_MODEL), row)] * 3,
        out_shape=[out, out, out],
        compiler_params=_cparams(("parallel",)),
        name="qkv_proj",
    )(x, g, w, cos, sin)


def _rope_tables(seq_len):
    inv = ROPE_THETA ** (-jnp.arange(0, HEAD_DIM, 2, dtype=F32) / HEAD_DIM)
    ang = jnp.arange(seq_len, dtype=F32)[:, None] * inv[None, :]
    cos = jnp.cos(ang)
    sin = jnp.sin(ang)
    cos_t = jnp.tile(jnp.concatenate([cos, cos], axis=1), (1, LANES // HEAD_DIM))
    sin_t = jnp.tile(jnp.concatenate([-sin, sin], axis=1), (1, LANES // HEAD_DIM))
    return cos_t, sin_t


def _pair_split(qp):
    lane = lax.broadcasted_iota(jnp.int32, qp.shape, 1)
    zero = jnp.zeros_like(qp)
    return jnp.concatenate([jnp.where(lane < HEAD_DIM, qp, zero),
                            jnp.where(lane >= HEAD_DIM, qp, zero)], axis=0)


def _pair_merge(o, m):
    lane = lax.broadcasted_iota(jnp.int32, (m, LANES), 1)
    return jnp.where(lane < HEAD_DIM, o[:m], o[m:])


NAT_ROWS_PER_STEP = 4
NAT_TQ = NAT_ROWS_PER_STEP * GRID_W
NAT_WIN_BLOCKS = 3
NAT_PAIRS_PER_STEP = 8


def _nat_body(q_ref, k0_ref, k1_ref, k2_ref, v0_ref, v1_ref, v2_ref, b_ref, o_ref):
    krefs = (k0_ref, k1_ref, k2_ref)
    vrefs = (v0_ref, v1_ref, v2_ref)
    for p in range(NAT_PAIRS_PER_STEP):
        cs = slice(p * LANES, (p + 1) * LANES)
        qs = _pair_split(q_ref[:, cs])
        s = [_dot_nt(qs, krefs[j][:, cs]) + b_ref[0, p, :, j * NAT_TQ:(j + 1) * NAT_TQ]
             for j in range(NAT_WIN_BLOCKS)]
        m = jnp.maximum(jnp.maximum(jnp.max(s[0], axis=-1, keepdims=True),
                                    jnp.max(s[1], axis=-1, keepdims=True)),
                        jnp.max(s[2], axis=-1, keepdims=True))
        e = [jnp.exp(sj - m) for sj in s]
        l = (jnp.sum(e[0], axis=-1, keepdims=True) + jnp.sum(e[1], axis=-1, keepdims=True)
             + jnp.sum(e[2], axis=-1, keepdims=True))
        r = 1.0 / l
        o = (_dot((e[0] * r).astype(BF16), vrefs[0][:, cs])
             + _dot((e[1] * r).astype(BF16), vrefs[1][:, cs])
             + _dot((e[2] * r).astype(BF16), vrefs[2][:, cs]))
        o_ref[:, cs] = _pair_merge(o, NAT_TQ).astype(BF16)


def _nat_bias_table(rpb):
    cols = jnp.arange(GRID_W)
    col_start = jnp.clip(cols - NA_WIN_W // 2, 0, GRID_W - NA_WIN_W)
    col_in = (cols[None, :] >= col_start[:, None]) & (cols[None, :] < col_start[:, None] + NA_WIN_W)
    dc_idx = jnp.clip(cols[None, :] - cols[:, None] + NA_WIN_W - 1, 0, 2 * NA_WIN_W - 2)
    tz = jnp.where(col_in[None, None], rpb.astype(F32)[:, :, dc_idx], NEG_INF)
    n_dr = 2 * NA_WIN_H - 1
    tz = jnp.concatenate([tz, jnp.full((NA_HEADS, 1, GRID_W, GRID_W), NEG_INF, F32)], axis=1)
    win_rows = NAT_WIN_BLOCKS * NAT_ROWS_PER_STEP
    half = NA_WIN_H // 2
    d_idx = np.full((3, NAT_ROWS_PER_STEP, win_rows), n_dr, np.int32)
    for case in range(3):
        for a in range(NAT_ROWS_PER_STEP):
            j0, d0 = ((0, NA_WIN_H - 1 - a), (a, NA_WIN_H - 1 - half), (half, NA_WIN_H - 1 - half - a))[case]
            for jj in range(NA_WIN_H):
                d_idx[case, a, j0 + jj] = d0 + jj
    t = tz[:, d_idx]
    t = jnp.transpose(t, (1, 0, 2, 4, 3, 5))
    return t.reshape(3, NA_HEADS // 2, 2 * NAT_TQ, win_rows * GRID_W)


def _nat_attention(q, k, v, bias, bsz, seq_len):
    n = q.shape[0]
    rows = seq_len // GRID_W
    nb = rows // NAT_ROWS_PER_STEP
    assert rows % NAT_ROWS_PER_STEP == 0 and nb >= NAT_WIN_BLOCKS
    pp = NAT_PAIRS_PER_STEP
    width = pp * LANES

    def win(j):
        return lambda hp, b, i: (b * nb + jnp.clip(i - 1, 0, nb - NAT_WIN_BLOCKS) + j, hp)

    def bias_map(hp, b, i):
        case = jnp.where(i == 0, 0, jnp.where(i == nb - 1, 2, 1))
        return (case, hp, 0, 0)

    qmap = lambda hp, b, i: (b * nb + i, hp)
    blk = lambda m: pl.BlockSpec((NAT_TQ, width), m)
    return pl.pallas_call(
        _nat_body,
        grid=(NA_HEADS // 2 // pp, bsz, nb),
        in_specs=[blk(qmap), blk(win(0)), blk(win(1)), blk(win(2)),
                  blk(win(0)), blk(win(1)), blk(win(2)),
                  pl.BlockSpec((1, pp, 2 * NAT_TQ, NAT_WIN_BLOCKS * NAT_TQ), bias_map)],
        out_specs=blk(qmap),
        out_shape=jax.ShapeDtypeStruct((n, D_MODEL), BF16),
        compiler_params=_cparams(("parallel", "parallel", "parallel")),
        name="nat_attention",
    )(q, k, k, k, v, v, v, bias)


def _swa_body(sink_ref, q_ref, k0_ref, k1_ref, k2_ref, v0_ref, v1_ref, v2_ref, o_ref, *, nblk):
    i = pl.program_id(1)
    tb = SW_BLOCK
    group = SW_HEADS // SW_KV_HEADS
    qq = lax.broadcasted_iota(jnp.int32, (group * tb, 3 * tb), 0) % tb
    kk = lax.broadcasted_iota(jnp.int32, (group * tb, 3 * tb), 1) - tb
    valid = ((jnp.abs(kk - qq) <= tb) & ((kk >= 0) | (i > 0)) & ((kk < tb) | (i < nblk - 1)))
    row = lax.broadcasted_iota(jnp.int32, (group * tb, 1), 0)
    for u in range(SW_KV_HEADS):
        pairs = [slice((2 * u + t) * LANES, (2 * u + t + 1) * LANES) for t in range(group // 2)]
        qs = jnp.concatenate([_pair_split(q_ref[:, cs]) for cs in pairs], axis=0)
        kc = jnp.concatenate([k0_ref[:, pairs[0]], k1_ref[:, pairs[0]], k2_ref[:, pairs[0]]], axis=0)
        vc = jnp.concatenate([v0_ref[:, pairs[0]], v1_ref[:, pairs[0]], v2_ref[:, pairs[0]]], axis=0)
        s = jnp.where(valid, _dot_nt(qs, kc), NEG_INF)
        sink = sink_ref[group * u + group - 1]
        for h in reversed(range(group - 1)):
            sink = jnp.where(row < (h + 1) * tb, sink_ref[group * u + h], sink)
        m = jnp.maximum(jnp.max(s, axis=-1, keepdims=True), sink)
        e = jnp.exp(s - m)
        l = jnp.sum(e, axis=-1, keepdims=True) + jnp.exp(sink - m)
        o = _dot((e / l).astype(BF16), vc)
        for t, cs in enumerate(pairs):
            o_ref[:, cs] = _pair_merge(o[2 * t * tb:(2 * t + 2) * tb], tb).astype(BF16)


def _swa_attention(q, k, v, sink, bsz, seq_len):
    n = q.shape[0]
    nblk = seq_len // SW_BLOCK
    cur = lambda b, i: (b * nblk + i, 0)
    prev = lambda b, i: (b * nblk + jnp.maximum(i - 1, 0), 0)
    nxt = lambda b, i: (b * nblk + jnp.minimum(i + 1, nblk - 1), 0)
    blk = lambda m: pl.BlockSpec((SW_BLOCK, D_MODEL), m)
    return pl.pallas_call(
        functools.partial(_swa_body, nblk=nblk),
        grid=(bsz, nblk),
        in_specs=[pl.BlockSpec(memory_space=pltpu.SMEM),
                  blk(cur), blk(prev), blk(cur), blk(nxt), blk(prev), blk(cur), blk(nxt)],
        out_specs=blk(cur),
        out_shape=jax.ShapeDtypeStruct((n, D_MODEL), BF16),
        compiler_params=_cparams(("parallel", "parallel")),
        name="swa_attention",
    )(sink.astype(F32), q, k, k, k, v, v, v)


DIFF_TQ = 2048
DIFF_TK = 2048
DIFF_ROW_CHUNKS = 4


def _diff_body(q_ref, k_ref, v_ref, lam_ref, subln_ref, o_ref, qs_sc, m_sc, l_sc, acc_sc,
               *, lambda_init):
    kv = pl.program_id(3)
    tq = q_ref.shape[0]
    tk = k_ref.shape[0]

    @pl.when(kv == 0)
    def _():
        qs_sc[...] = _pair_split(q_ref[...])
        m_sc[...] = jnp.full(m_sc.shape, NEG_INF, F32)
        l_sc[...] = jnp.zeros(l_sc.shape, F32)
        acc_sc[...] = jnp.zeros(acc_sc.shape, F32)

    k = k_ref[...]
    v = v_ref[...]
    rows = 2 * tq // DIFF_ROW_CHUNKS
    for c in range(DIFF_ROW_CHUNKS):
        rs = slice(c * rows, (c + 1) * rows)
        s = _dot_nt(qs_sc[rs, :], k)
        m_prev = m_sc[rs, :]
        m_new = jnp.maximum(m_prev, jnp.max(s, axis=-1, keepdims=True))
        alpha = jnp.exp(m_prev - m_new)
        p = jnp.exp(s - jnp.concatenate([m_new] * (tk // LANES), axis=1))
        l_sc[rs, :] = alpha * l_sc[rs, :] + jnp.sum(p, axis=-1, keepdims=True)
        acc_sc[rs, :] = alpha * acc_sc[rs, :] + _dot(p.astype(BF16), v)
        m_sc[rs, :] = m_new

    @pl.when(kv == pl.num_programs(3) - 1)
    def _():
        o12 = acc_sc[...] / l_sc[...]
        lam4 = lam_ref[...]
        lam = (jnp.exp(jnp.sum(lam4[0:1] * lam4[1:2], axis=-1, keepdims=True))
               - jnp.exp(jnp.sum(lam4[2:3] * lam4[3:4], axis=-1, keepdims=True)) + lambda_init)
        o = o12[:tq] - lam * o12[tq:]
        o = _rms(o, subln_ref[...]) * (1.0 - lambda_init)
        o_ref[...] = o.astype(BF16)


def _diff_attention(q, k, v, lam4, subln, bsz, seq_len, lambda_init):
    n = q.shape[0]
    tq = min(DIFF_TQ, seq_len)
    tk = min(DIFF_TK, seq_len)
    nq = seq_len // tq
    nk = seq_len // tk
    return pl.pallas_call(
        functools.partial(_diff_body, lambda_init=lambda_init),
        grid=(bsz, DIFF_HEADS, nq, nk),
        in_specs=[
            pl.BlockSpec((tq, LANES), lambda b, h, qi, ki: (b * nq + qi, h)),
            pl.BlockSpec((tk, LANES), lambda b, h, qi, ki: (b * nk + ki, h)),
            pl.BlockSpec((tk, LANES), lambda b, h, qi, ki: (b * nk + ki, h)),
            pl.BlockSpec((4, HEAD_DIM), lambda b, h, qi, ki: (0, 0)),
            pl.BlockSpec((1, LANES), lambda b, h, qi, ki: (0, 0)),
        ],
        out_specs=pl.BlockSpec((tq, LANES), lambda b, h, qi, ki: (b * nq + qi, h)),
        out_shape=jax.ShapeDtypeStruct((n, D_MODEL), BF16),
        scratch_shapes=[pltpu.VMEM((2 * tq, LANES), BF16), pltpu.VMEM((2 * tq, LANES), F32),
                        pltpu.VMEM((2 * tq, LANES), F32), pltpu.VMEM((2 * tq, LANES), F32)],
        compiler_params=_cparams(("parallel", "parallel", "parallel", "arbitrary")),
        name="diff_attention",
    )(q, k, v, lam4, subln)


def _oproj_body(o_ref, x_ref, wo_ref, g_ref, wr_ref, xn_ref, aff_ref):
    xn = x_ref[...] + _dot(o_ref[...], wo_ref[...])
    xn_ref[...] = xn
    h = _rms(xn, g_ref[...]).astype(BF16)
    logits = _dot_nt(wr_ref[...], h)
    m = jnp.max(logits, axis=0, keepdims=True)
    e = jnp.exp(logits - m)
    aff_ref[...] = e / jnp.sum(e, axis=0, keepdims=True)


OPROJ_TM = 1024


def _oproj_router(o, x, wo, g, wr_t):
    n = x.shape[0]
    tm = OPROJ_TM
    row = lambda i: (i, 0)
    const = lambda i: (0, 0)
    return pl.pallas_call(
        _oproj_body,
        grid=(n // tm,),
        in_specs=[pl.BlockSpec((tm, D_MODEL), row), pl.BlockSpec((tm, D_MODEL), row),
                  pl.BlockSpec((D_MODEL, D_MODEL), const), pl.BlockSpec((1, D_MODEL), const),
                  pl.BlockSpec((N_EXPERTS, D_MODEL), const)],
        out_specs=[pl.BlockSpec((tm, D_MODEL), row), pl.BlockSpec((N_EXPERTS, tm), lambda i: (0, i))],
        out_shape=[jax.ShapeDtypeStruct((n, D_MODEL), F32),
                   jax.ShapeDtypeStruct((N_EXPERTS, n), F32)],
        compiler_params=_cparams(("parallel",)),
        name="oproj_router",
    )(o, x, wo, g, wr_t)


def _route_body(a_ref, idx_ref, gate_ref, pos_ref, off_ref, *, cap):
    e_id = pl.program_id(0)
    a = a_ref[0]
    nr = a.shape[0]
    bits = pltpu.bitcast(a, jnp.int32)

    def count(mask):
        c = jnp.sum(mask.astype(F32), axis=1, keepdims=True)
        return jnp.sum(c, axis=0, keepdims=True)

    def search(step, thr):
        cand = thr | lax.shift_left(jnp.int32(1), 30 - step)
        return jnp.where(count(bits >= cand) >= cap, cand, thr)

    thr = lax.fori_loop(0, 31, search, jnp.zeros((1, 1), jnp.int32))
    gt = bits > thr
    eq = bits == thr
    need = cap - count(gt)

    li = lax.broadcasted_iota(jnp.int32, (LANES, LANES), 0)
    lj = lax.broadcasted_iota(jnp.int32, (LANES, LANES), 1)
    upper_incl = (li <= lj).astype(BF16)
    ri = lax.broadcasted_iota(jnp.int32, (nr, nr), 0)
    rj = lax.broadcasted_iota(jnp.int32, (nr, nr), 1)
    lower_strict = (rj < ri).astype(BF16)

    def prefix(mask):
        w = _dot(mask.astype(BF16), upper_incl)
        tot = jnp.broadcast_to(w[:, LANES - 1:LANES], (nr, LANES))
        off = _dot(lower_strict, tot.astype(BF16))
        return w, tot, off

    w_eq, _, off_eq = prefix(eq)
    sel = gt | (eq & (off_eq + w_eq <= need))
    w, tot, off = prefix(sel)
    base = (e_id * cap).astype(F32)
    pos_ref[0] = jnp.where(sel, base + off + w - 1.0, -1.0)
    off_ref[0] = off

    off_excl = off[:, 0:1]
    off_incl = off_excl + tot[:, 0:1]
    w_t = w.T.astype(BF16)
    a_t = a.T
    a_hi_t = a_t.astype(BF16)
    r1 = a_t - a_hi_t.astype(F32)
    a_mid_t = r1.astype(BF16)
    a_lo_t = (r1 - a_mid_t.astype(F32)).astype(BF16)
    st = idx_ref.shape[2]
    r_col = lax.broadcasted_iota(jnp.int32, (nr, st), 0).astype(F32)
    l_col = lax.broadcasted_iota(jnp.int32, (LANES, st), 0).astype(F32)

    def slot_tile(t, carry):
        j = (t * st + lax.broadcasted_iota(jnp.int32, (1, st), 1)).astype(F32)
        hit = (off_excl <= j) & (j < off_incl)
        hit_b = hit.astype(BF16)
        k = j - jnp.sum(jnp.where(hit, off_excl, 0.0), axis=0, keepdims=True)
        row = jnp.sum(jnp.where(hit, r_col, 0.0), axis=0, keepdims=True)
        wsel = _dot(w_t, hit_b)
        lane = jnp.sum((wsel <= k).astype(F32), axis=0, keepdims=True)
        asel = _dot(a_hi_t, hit_b) + _dot(a_mid_t, hit_b) + _dot(a_lo_t, hit_b)
        gate = jnp.sum(jnp.where(l_col == lane, asel, 0.0), axis=0, keepdims=True)
        idx_ref[0, pl.ds(t, 1), :] = (row * LANES + lane).astype(jnp.int32)
        gate_ref[0, pl.ds(t, 1), :] = gate
        return carry

    lax.fori_loop(0, cap // st, slot_tile, 0)


ROUTE_SLOT_TILE = 1024


def _route(aff_t, cap):
    n = aff_t.shape[1]
    nr = n // LANES
    st = min(ROUTE_SLOT_TILE, cap)
    a3 = aff_t.reshape(N_EXPERTS, nr, LANES)
    blk = lambda r, c=LANES: pl.BlockSpec((1, r, c), lambda e: (e, 0, 0))
    return pl.pallas_call(
        functools.partial(_route_body, cap=cap),
        grid=(N_EXPERTS,),
        in_specs=[blk(nr)],
        out_specs=[blk(cap // st, st), blk(cap // st, st), blk(nr), blk(nr)],
        out_shape=[jax.ShapeDtypeStruct((N_EXPERTS, cap // st, st), jnp.int32),
                   jax.ShapeDtypeStruct((N_EXPERTS, cap // st, st), F32),
                   jax.ShapeDtypeStruct((N_EXPERTS, nr, LANES), F32),
                   jax.ShapeDtypeStruct((N_EXPERTS, nr, LANES), F32)],
        compiler_params=_cparams(("parallel",)),
        name="ec_route",
    )(a3)


FFN_TC = 512
FFN_FCHUNK = 512


def _ffn_body(idx_cur, idx_nxt, x_hbm, g_ref, gate_ref, wg_ref, wu_ref, wd_ref, y_ref,
              xbuf, xe_sc, sem):
    tc = xbuf.shape[0]
    step = pl.program_id(0) * pl.num_programs(1) + pl.program_id(1)
    total = pl.num_programs(0) * pl.num_programs(1)

    def row_copy(idx, r):
        return pltpu.make_async_copy(x_hbm.at[pl.ds(idx[0, 0, r], 1)], xbuf.at[pl.ds(r, 1)],
                                     sem.at[0])

    def wait_rows():
        pltpu.make_async_copy(x_hbm.at[pl.ds(0, tc)], xbuf, sem.at[0]).wait()

    @pl.when(step == 0)
    def _():
        def one(r, c):
            row_copy(idx_cur, r).start()
            return c
        lax.fori_loop(0, tc, one, 0)

    wait_rows()
    xe_sc[...] = _rms(xbuf[...], g_ref[...]).astype(BF16)
    acc = jnp.zeros((tc, D_MODEL), F32)
    nchunk = D_EXPERT // FFN_FCHUNK
    rows_per_chunk = tc // nchunk
    for c in range(nchunk):
        for r in range(c * rows_per_chunk, (c + 1) * rows_per_chunk):
            row_copy(idx_nxt, r).start(priority=r % 2)
        fs = slice(c * FFN_FCHUNK, (c + 1) * FFN_FCHUNK)
        xe = xe_sc[...]
        hg = _dot(xe, wg_ref[0, :, fs])
        hu = _dot(xe, wu_ref[0, :, fs])
        hid = (hg * jax.nn.sigmoid(hg)) * hu
        acc = acc + _dot(hid.astype(BF16), wd_ref[0, fs, :])
    y_ref[...] = (acc * gate_ref[0]).astype(BF16)

    @pl.when(step == total - 1)
    def _():
        wait_rows()


def _expert_ffn(x, g, idx, gate, wg, wu, wd, cap):
    tc = min(FFN_TC, cap)
    nt = cap // tc
    ntiles = N_EXPERTS * nt
    idx3 = idx.reshape(ntiles, 1, tc)
    gate3 = gate.reshape(N_EXPERTS, cap, 1)
    smem = lambda m: pl.BlockSpec((1, 1, tc), m, memory_space=pltpu.SMEM)
    wspec = lambda a, b: pl.BlockSpec((1, a, b), lambda e, i: (e, 0, 0))
    return pl.pallas_call(
        _ffn_body,
        grid=(N_EXPERTS, nt),
        in_specs=[smem(lambda e, i: (e * nt + i, 0, 0)),
                  smem(lambda e, i: (jnp.minimum(e * nt + i + 1, ntiles - 1), 0, 0)),
                  pl.BlockSpec(memory_space=pl.ANY),
                  pl.BlockSpec((1, D_MODEL), lambda e, i: (0, 0)),
                  pl.BlockSpec((1, tc, 1), lambda e, i: (e, i, 0)),
                  wspec(D_MODEL, D_EXPERT), wspec(D_MODEL, D_EXPERT), wspec(D_EXPERT, D_MODEL)],
        out_specs=pl.BlockSpec((tc, D_MODEL), lambda e, i: (e * nt + i, 0)),
        out_shape=jax.ShapeDtypeStruct((N_EXPERTS * cap, D_MODEL), BF16),
        scratch_shapes=[pltpu.VMEM((tc, D_MODEL), F32), pltpu.VMEM((tc, D_MODEL), BF16),
                        pltpu.SemaphoreType.DMA((1,))],
        compiler_params=_cparams(("arbitrary", "arbitrary")),
        name="expert_ffn",
    )(idx3, idx3, x, g, gate3, wg, wu, wd)


COMB_TB = 512
COMB_CW = 128
COMB_ALIGN = 16


def _combine_body(boff_ref, x_ref, pos_ref, y_hbm, gfin_ref, o_ref, win, xwin, sem, xsem,
                  *, cap, final_norm):
    b = pl.program_id(0)
    total = N_EXPERTS * cap
    tb = x_ref.shape[0]
    cw = COMB_CW

    slot = b % 2

    def wstart(e, blk):
        s = jnp.bitwise_and(e * cap + boff_ref[e, blk], -COMB_ALIGN)
        return pl.multiple_of(jnp.minimum(s, total - cw), COMB_ALIGN)

    def window_copy(e, start, dst_slot):
        return pltpu.make_async_copy(y_hbm.at[pl.ds(start, cw)],
                                     win.at[dst_slot, pl.ds(e * cw, cw)], sem.at[dst_slot, e])

    @pl.when(b == 0)
    def _():
        for e in range(N_EXPERTS):
            window_copy(e, wstart(e, 0), 0).start()

    @pl.when(b + 1 < pl.num_programs(0))
    def _():
        for e in range(N_EXPERTS):
            window_copy(e, wstart(e, b + 1), 1 - slot).start()

    starts = [wstart(e, b) for e in range(N_EXPERTS)]
    wrow = lax.broadcasted_iota(jnp.int32, (cw, tb), 0).astype(F32)
    pos = pos_ref[...]
    onehot_t = jnp.concatenate(
        [(pos[e:e + 1, :] - starts[e].astype(F32) == wrow).astype(BF16) for e in range(N_EXPERTS)],
        axis=0)
    for e in range(N_EXPERTS):
        window_copy(e, starts[e], slot).wait()
    o_ref[...] = x_ref[...] + _dot_tn(onehot_t, win[slot])

    for e in range(N_EXPERTS):
        end = e * cap + boff_ref[e, b + 1]
        extra = lax.shift_right_logical(jnp.maximum(end - starts[e] - 1, 0), int(math.log2(cw)))

        def more(c, carry, e=e):
            lo = starts[e] + c * cw
            s = pl.multiple_of(jnp.minimum(lo, total - cw), COMB_ALIGN)
            cp = pltpu.make_async_copy(y_hbm.at[pl.ds(s, cw)], xwin, xsem.at[0])
            cp.start()
            cp.wait()
            pe = pos[e:e + 1, :]
            oh = ((pe - s.astype(F32) == wrow) & (pe >= lo.astype(F32))).astype(BF16)
            o_ref[...] += _dot_tn(oh, xwin[...])
            return carry

        lax.fori_loop(1, extra + 1, more, 0)

    if final_norm:
        o_ref[...] = _rms(o_ref[...], gfin_ref[...])


def _combine(x, pos_t, boff, y, gfin, cap, final_norm):
    n = x.shape[0]
    tb = COMB_TB
    grid_spec = pltpu.PrefetchScalarGridSpec(
        num_scalar_prefetch=1,
        grid=(n // tb,),
        in_specs=[pl.BlockSpec((tb, D_MODEL), lambda b, s: (b, 0)),
                  pl.BlockSpec((N_EXPERTS, tb), lambda b, s: (0, b)),
                  pl.BlockSpec(memory_space=pl.ANY),
                  pl.BlockSpec((1, D_MODEL), lambda b, s: (0, 0))],
        out_specs=pl.BlockSpec((tb, D_MODEL), lambda b, s: (b, 0)),
        scratch_shapes=[pltpu.VMEM((2, N_EXPERTS * COMB_CW, D_MODEL), BF16),
                        pltpu.VMEM((COMB_CW, D_MODEL), BF16),
                        pltpu.SemaphoreType.DMA((2, N_EXPERTS)),
                        pltpu.SemaphoreType.DMA((1,))],
    )
    return pl.pallas_call(
        functools.partial(_combine_body, cap=cap, final_norm=final_norm),
        grid_spec=grid_spec,
        out_shape=jax.ShapeDtypeStruct((n, D_MODEL), F32),
        compiler_params=_cparams(("arbitrary",)),
        name="ec_combine",
    )(boff, x, pos_t, y, gfin)


def _moe(xn, aff_t, ffn_norm, wg, wu, wd, gfin, final_norm):
    n = xn.shape[0]
    cap = EC_CAPACITY_FACTOR * n // N_EXPERTS
    idx, gate, pos, off = _route(aff_t, cap)
    y = _expert_ffn(xn, ffn_norm, idx, gate, wg, wu, wd, cap)
    pos_t = pos.reshape(N_EXPERTS, n)
    rows_per_blk = COMB_TB // LANES
    boff = off[:, ::rows_per_blk, 0].astype(jnp.int32)
    boff = jnp.concatenate([boff, jnp.full((N_EXPERTS, 1), cap, jnp.int32)], axis=1)
    return _combine(xn, pos_t, boff, y, gfin, cap, final_norm)


def _expand_kv_columns(w_qkv):
    qd = SW_HEADS * HEAD_DIM
    kd = SW_KV_HEADS * HEAD_DIM
    head_of = jnp.arange(SW_HEADS) // (SW_HEADS // SW_KV_HEADS)
    cols = (head_of[:, None] * HEAD_DIM + jnp.arange(HEAD_DIM)[None, :]).reshape(-1)
    return jnp.concatenate([w_qkv[:, :qd], w_qkv[:, qd:qd + kd][:, cols],
                            w_qkv[:, qd + kd:][:, cols]], axis=1)


def _run_group(x3, layers, final_norm_g):
    bsz, seq_len, _ = x3.shape
    n = bsz * seq_len
    x = x3.reshape(n, D_MODEL)
    cos_t, sin_t = _rope_tables(seq_len)
    gfin = final_norm_g.reshape(1, D_MODEL).astype(F32)
    for i, lp in enumerate(layers):
        kind = MIXER_PATTERN[i % len(MIXER_PATTERN)]
        q, k, v = _qkv_proj(x, lp["mix_norm"], lp["w_qkv"], cos_t, sin_t, seq_len, kind != "nat")
        if kind == "nat":
            o = _nat_attention(q, k, v, lp["nat_bias"], bsz, seq_len)
        elif kind == "swa":
            o = _swa_attention(q, k, v, lp["sink"], bsz, seq_len)
        else:
            lambda_init = 0.8 - 0.6 * math.exp(-0.3 * i)
            o = _diff_attention(q, k, v, lp["lam4"], lp["subln"], bsz, seq_len, lambda_init)
        xn, aff_t = _oproj_router(o, x, lp["w_o"], lp["ffn_norm"], lp["w_router_t"])
        x = _moe(xn, aff_t, lp["ffn_norm"], lp["w_gate"], lp["w_up"], lp["w_down"],
                 gfin, i == len(layers) - 1)
    return x.reshape(bsz, seq_len, D_MODEL)


def _prepare_layer(i, lp):
    kind = MIXER_PATTERN[i % len(MIXER_PATTERN)]
    out = dict(
        mix_norm=lp["mix_norm"].reshape(1, D_MODEL).astype(F32),
        ffn_norm=lp["ffn_norm"].reshape(1, D_MODEL).astype(F32),
        w_o=lp["w_o"].astype(BF16),
        w_router_t=lp["w_router"].T.astype(BF16),
        w_gate=lp["w_gate"].astype(BF16), w_up=lp["w_up"].astype(BF16),
        w_down=lp["w_down"].astype(BF16),
    )
    if kind == "swa":
        out["w_qkv"] = _expand_kv_columns(lp["w_qkv"]).astype(BF16)
        out["sink"] = lp["sink"]
    else:
        out["w_qkv"] = lp["w_qkv"].astype(BF16)
    if kind == "nat":
        out["nat_bias"] = _nat_bias_table(lp["rpb"])
    if kind == "diff":
        out["lam4"] = jnp.stack([lp["lambda_q1"], lp["lambda_k1"], lp["lambda_q2"],
                                 lp["lambda_k2"]]).astype(F32)
        out["subln"] = lp["subln"].reshape(1, LANES).astype(F32)
    return out


def kernel(x_prompt, x_sample, l0_mix_norm, l0_w_qkv, l0_rpb, l0_w_o, l0_ffn_norm, l0_w_router, l0_w_gate, l0_w_up, l0_w_down, l1_mix_norm, l1_w_qkv, l1_sink, l1_w_o, l1_ffn_norm, l1_w_router, l1_w_gate, l1_w_up, l1_w_down, l2_mix_norm, l2_w_qkv, l2_lambda_q1, l2_lambda_k1, l2_lambda_q2, l2_lambda_k2, l2_subln, l2_w_o, l2_ffn_norm, l2_w_router, l2_w_gate, l2_w_up, l2_w_down, l3_mix_norm, l3_w_qkv, l3_rpb, l3_w_o, l3_ffn_norm, l3_w_router, l3_w_gate, l3_w_up, l3_w_down, final_norm):
    layers = [
        dict(mix_norm=l0_mix_norm, w_qkv=l0_w_qkv, rpb=l0_rpb, w_o=l0_w_o, ffn_norm=l0_ffn_norm,
             w_router=l0_w_router, w_gate=l0_w_gate, w_up=l0_w_up, w_down=l0_w_down),
        dict(mix_norm=l1_mix_norm, w_qkv=l1_w_qkv, sink=l1_sink, w_o=l1_w_o, ffn_norm=l1_ffn_norm,
             w_router=l1_w_router, w_gate=l1_w_gate, w_up=l1_w_up, w_down=l1_w_down),
        dict(mix_norm=l2_mix_norm, w_qkv=l2_w_qkv, lambda_q1=l2_lambda_q1, lambda_k1=l2_lambda_k1,
             lambda_q2=l2_lambda_q2, lambda_k2=l2_lambda_k2, subln=l2_subln, w_o=l2_w_o,
             ffn_norm=l2_ffn_norm, w_router=l2_w_router, w_gate=l2_w_gate, w_up=l2_w_up,
             w_down=l2_w_down),
        dict(mix_norm=l3_mix_norm, w_qkv=l3_w_qkv, rpb=l3_rpb, w_o=l3_w_o, ffn_norm=l3_ffn_norm,
             w_router=l3_w_router, w_gate=l3_w_gate, w_up=l3_w_up, w_down=l3_w_down),
    ]
    layers = [_prepare_layer(i, lp) for i, lp in enumerate(layers)]
    y_prompt = _run_group(x_prompt, layers, final_norm)
    y_sample = _run_group(x_sample, layers, final_norm)
    return (y_prompt, y_sample)
```
